```python
import math
import jax, jax.numpy as jnp
from jax import lax
import numpy as np

D_MODEL = 1024
BATCH = 4
SEQ = 8192
DEPTH = 1
DEC_BATCH = 32
DEC_SEQ = 8
PAST_LEN = 16384
PAGE_SIZE = 128

N_HEADS = 16
HEAD_DIM = 64
ATTN_WIDTH = N_HEADS * HEAD_DIM
CONV_CH = D_MODEL
CONV_K = 31
D_FF = 2816
Q_BLOCK = 128
FORGET_BIAS_LO = 4.0
FORGET_BIAS_HI = 12.0
EPS = 1e-6
IN_COLS = 2 * CONV_CH + 3 * ATTN_WIDTH + N_HEADS + 2 * D_MODEL

kernel_name = "macaron_conv_fox_gated_hybrid_step"


def rmsnorm(x, g):
    xf = x.astype(jnp.float32)
    y = xf * lax.rsqrt(jnp.mean(xf * xf, axis=-1, keepdims=True) + EPS)
    return (y * g.astype(jnp.float32)).astype(x.dtype)


def swiglu(x, w_gate, w_up, w_down):
    return (jax.nn.silu(x @ w_gate) * (x @ w_up)) @ w_down


def project_inputs(x, norm_ffn1, w1_gate, w1_up, w1_down, norm_mix, w_in, b_forget):
    x = x + 0.5 * swiglu(rmsnorm(x, norm_ffn1), w1_gate, w1_up, w1_down)
    z = rmsnorm(x, norm_mix) @ w_in
    sizes = [CONV_CH, CONV_CH, ATTN_WIDTH, ATTN_WIDTH, ATTN_WIDTH, N_HEADS, D_MODEL]
    idx = []
    acc = 0
    for s in sizes:
        acc += s
        idx.append(acc)
    glu_a, glu_b, q, k, v, f, gate_conv, gate_attn = jnp.split(z, idx, axis=-1)
    b, t = x.shape[0], x.shape[1]
    glu = glu_a * jax.nn.sigmoid(glu_b)
    q = q.reshape(b, t, N_HEADS, HEAD_DIM)
    k = k.reshape(b, t, N_HEADS, HEAD_DIM)
    v = v.reshape(b, t, N_HEADS, HEAD_DIM)
    logf = jax.nn.log_sigmoid(f.astype(jnp.float32) + b_forget.astype(jnp.float32))
    return x, glu, q, k, v, logf, gate_conv, gate_attn


def conv_branch(buf, w_dw, b_dw, ln_g, ln_b, w_proj_conv):
    y = lax.conv_general_dilated(buf, w_dw[:, None, :], window_strides=(1,), padding='VALID',
                                 dimension_numbers=('NWC', 'WIO', 'NWC'),
                                 feature_group_count=buf.shape[-1])
    y = (y + b_dw).astype(jnp.float32)
    mu = jnp.mean(y, axis=-1, keepdims=True)
    var = jnp.mean(jnp.square(y - mu), axis=-1, keepdims=True)
    yn = (y - mu) * lax.rsqrt(var + EPS) * ln_g.astype(jnp.float32) + ln_b.astype(jnp.float32)
    return jax.nn.silu(yn.astype(buf.dtype)) @ w_proj_conv


def fox_prompt(q, k, v, logf):
    b, t, h, dh = q.shape
    nb = t // Q_BLOCK
    scale = 1.0 / math.sqrt(dh)
    c = jnp.cumsum(logf, axis=1).transpose(0, 2, 1)
    key_pos = jnp.arange(t)
    q_blocks = q.reshape(b, nb, Q_BLOCK, h, dh).swapaxes(0, 1)
    c_blocks = c.reshape(b, h, nb, Q_BLOCK).transpose(2, 0, 1, 3)

    def one_block(args):
        blk, q_i, c_i = args
        s = jnp.einsum('bqhd,bkhd->bhqk', q_i, k).astype(jnp.float32) * scale
        s = s + c_i[..., :, None] - c[..., None, :]
        q_pos = blk * Q_BLOCK + jnp.arange(Q_BLOCK)
        s = jnp.where(key_pos[None, :] <= q_pos[:, None], s, -jnp.inf)
        p = jax.nn.softmax(s, axis=-1).astype(v.dtype)
        return jnp.einsum('bhqk,bkhd->bqhd', p, v)

    out = lax.map(one_block, (jnp.arange(nb), q_blocks, c_blocks))
    return out.swapaxes(0, 1).reshape(b, t, h * dh)


def fox_sample(q, k, v, logf, k_past, v_past, logf_past):
    b, s_len, h, dh = q.shape
    p_len = k_past.shape[1]
    scale = 1.0 / math.sqrt(dh)
    lp = logf_past.astype(jnp.float32)
    r_past = (lax.cumsum(lp, axis=1, reverse=True) - lp).transpose(0, 2, 1)
    c_new = jnp.cumsum(logf, axis=1).transpose(0, 2, 1)
    s_past = jnp.einsum('bqhd,bkhd->bhqk', q, k_past).astype(jnp.float32) * scale
    s_past = s_past + c_new[..., :, None] + r_past[..., None, :]
    s_new = jnp.einsum('bqhd,bkhd->bhqk', q, k).astype(jnp.float32) * scale
    s_new = s_new + c_new[..., :, None] - c_new[..., None, :]
    pos = jnp.arange(s_len)
    s_new = jnp.where(pos[None, :] <= pos[:, None], s_new, -jnp.inf)
    p = jax.nn.softmax(jnp.concatenate([s_past, s_new], axis=-1), axis=-1).astype(v.dtype)
    out = (jnp.einsum('bhqk,bkhd->bqhd', p[..., :p_len], v_past)
           + jnp.einsum('bhqk,bkhd->bqhd', p[..., p_len:], v))
    return out.reshape(b, s_len, h * dh)


def merge_and_finish(x, conv_y, attn_heads, gate_conv, gate_attn, w_proj_attn, w_out,
                     norm_ffn2, w2_gate, w2_up, w2_down):
    attn_y = attn_heads @ w_proj_attn
    m = jax.nn.sigmoid(gate_conv) * conv_y + jax.nn.sigmoid(gate_attn) * attn_y
    x = x + m @ w_out
    return x + 0.5 * swiglu(rmsnorm(x, norm_ffn2), w2_gate, w2_up, w2_down)


def setup_inputs(seed: int = 0) -> dict:
    key = jax.random.key(seed)
    ks = jax.random.split(key, 32)
    f32 = jnp.float32
    n_pages = PAST_LEN // PAGE_SIZE
    n_used = DEC_BATCH * n_pages
    n_pool = n_used + max(1, n_used // 4)

    def nrm(k, shape, scale):
        return jax.random.normal(k, shape, f32) * scale

    def gain(k, shape):
        return 1.0 + 0.05 * jax.random.normal(k, shape, f32)

    head_bias = jnp.linspace(FORGET_BIAS_LO, FORGET_BIAS_HI, N_HEADS, dtype=f32)
    x_prompt = nrm(ks[0], (BATCH, SEQ, D_MODEL), 1.0)
    x_sample = nrm(ks[1], (DEC_BATCH, DEC_SEQ, D_MODEL), 1.0)
    cache_k = nrm(ks[2], (DEPTH, n_pool, PAGE_SIZE, N_HEADS, HEAD_DIM), 1.0)
    cache_v = nrm(ks[3], (DEPTH, n_pool, PAGE_SIZE, N_HEADS, HEAD_DIM), 1.0)
    cache_logf = jax.nn.log_sigmoid(head_bias + jax.random.normal(ks[4], (DEPTH, n_pool, PAGE_SIZE, N_HEADS), f32))
    state_conv = nrm(ks[5], (DEPTH, DEC_BATCH, CONV_K - 1, CONV_CH), 0.5)
    page_table = jax.random.permutation(ks[6], n_pool)[:n_used].reshape(DEC_BATCH, n_pages).astype(jnp.int32)
    return {
        'x_prompt': x_prompt,
        'x_sample': x_sample,
        'cache_k': cache_k,
        'cache_v': cache_v,
        'cache_logf': cache_logf,
        'state_conv': state_conv,
        'page_table': page_table,
        'norm_ffn1': gain(ks[7], (DEPTH, D_MODEL)),
        'w_ffn1_gate': nrm(ks[8], (DEPTH, D_MODEL, D_FF), D_MODEL ** -0.5),
        'w_ffn1_up': nrm(ks[9], (DEPTH, D_MODEL, D_FF), D_MODEL ** -0.5),
        'w_ffn1_down': nrm(ks[10], (DEPTH, D_FF, D_MODEL), D_FF ** -0.5),
        'norm_mix': gain(ks[11], (DEPTH, D_MODEL)),
        'w_in': nrm(ks[12], (DEPTH, D_MODEL, IN_COLS), D_MODEL ** -0.5),
        'b_forget': head_bias + 0.1 * jax.random.normal(ks[13], (DEPTH, N_HEADS), f32),
        'w_dw': nrm(ks[14], (DEPTH, CONV_K, CONV_CH), CONV_K ** -0.5),
        'b_dw': nrm(ks[15], (DEPTH, CONV_CH), 0.02),
        'ln_conv_g': gain(ks[16], (DEPTH, CONV_CH)),
        'ln_conv_b': nrm(ks[17], (DEPTH, CONV_CH), 0.02),
        'w_proj_conv': nrm(ks[18], (DEPTH, CONV_CH, D_MODEL), CONV_CH ** -0.5),
        'w_proj_attn': nrm(ks[19], (DEPTH, ATTN_WIDTH, D_MODEL), ATTN_WIDTH ** -0.5),
        'w_out': nrm(ks[20], (DEPTH, D_MODEL, D_MODEL), D_MODEL ** -0.5),
        'norm_ffn2': gain(ks[21], (DEPTH, D_MODEL)),
        'w_ffn2_gate': nrm(ks[22], (DEPTH, D_MODEL, D_FF), D_MODEL ** -0.5),
        'w_ffn2_up': nrm(ks[23], (DEPTH, D_MODEL, D_FF), D_MODEL ** -0.5),
        'w_ffn2_down': nrm(ks[24], (DEPTH, D_FF, D_MODEL), D_FF ** -0.5),
        'norm_final': gain(ks[25], (D_MODEL,)),
    }


def reference(x_prompt, x_sample, cache_k, cache_v, cache_logf, state_conv, page_table,
              norm_ffn1, w_ffn1_gate, w_ffn1_up, w_ffn1_down, norm_mix, w_in, b_forget,
              w_dw, b_dw, ln_conv_g, ln_conv_b, w_proj_conv, w_proj_attn, w_out,
              norm_ffn2, w_ffn2_gate, w_ffn2_up, w_ffn2_down, norm_final):
    n_pages = PAST_LEN // PAGE_SIZE
    p_len = n_pages * PAGE_SIZE
    xp, xs = x_prompt, x_sample
    kp_l, vp_l, lfp_l, cvp_l, ks_l, vs_l, lfs_l, cvs_l = [], [], [], [], [], [], [], []
    for l in range(DEPTH):
        pin = (norm_ffn1[l], w_ffn1_gate[l], w_ffn1_up[l], w_ffn1_down[l], norm_mix[l], w_in[l], b_forget[l])
        pconv = (w_dw[l], b_dw[l], ln_conv_g[l], ln_conv_b[l], w_proj_conv[l])
        pout = (w_proj_attn[l], w_out[l], norm_ffn2[l], w_ffn2_gate[l], w_ffn2_up[l], w_ffn2_down[l])

        xp, glu_p, qp, kp, vp, lfp, gcp, gap = project_inputs(xp, *pin)
        buf_p = jnp.pad(glu_p, ((0, 0), (CONV_K - 1, 0), (0, 0)))
        conv_p = conv_branch(buf_p, *pconv)
        attn_p = fox_prompt(qp, kp, vp, lfp)
        xp = merge_and_finish(xp, conv_p, attn_p, gcp, gap, *pout)

        xs, glu_s, qs, ks_new, vs_new, lfs, gcs, gas = project_inputs(xs, *pin)
        buf_s = jnp.concatenate([state_conv[l].astype(glu_s.dtype), glu_s], axis=1)
        conv_s = conv_branch(buf_s, *pconv)
        k_past = cache_k[l][page_table].reshape(DEC_BATCH, p_len, N_HEADS, HEAD_DIM)
        v_past = cache_v[l][page_table].reshape(DEC_BATCH, p_len, N_HEADS, HEAD_DIM)
        lf_past = cache_logf[l][page_table].reshape(DEC_BATCH, p_len, N_HEADS)
        attn_s = fox_sample(qs, ks_new, vs_new, lfs, k_past, v_past, lf_past)
        xs = merge_and_finish(xs, conv_s, attn_s, gcs, gas, *pout)

        kp_l.append(kp)
        vp_l.append(vp)
        lfp_l.append(lfp)
        cvp_l.append(glu_p[:, -(CONV_K - 1):])
        ks_l.append(ks_new)
        vs_l.append(vs_new)
        lfs_l.append(lfs)
        cvs_l.append(buf_s[:, -(CONV_K - 1):])

    y_prompt = rmsnorm(xp, norm_final)
    y_sample = rmsnorm(xs, norm_final)
    k_prompt = jnp.stack(kp_l)
    v_prompt = jnp.stack(vp_l)
    logf_prompt = jnp.stack(lfp_l)
    conv_prompt = jnp.stack(cvp_l)
    k_sample = jnp.stack(ks_l)
    v_sample = jnp.stack(vs_l)
    logf_sample = jnp.stack(lfs_l)
    conv_sample = jnp.stack(cvs_l)
    return (y_prompt, y_sample, k_prompt, v_prompt, logf_prompt, conv_prompt,
            k_sample, v_sample, logf_sample, conv_sample)
```

```python
import functools
import math

import jax
import jax.numpy as jnp
from jax import lax
from jax.experimental import pallas as pl
from jax.experimental.pallas import tpu as pltpu

F32 = jnp.float32
BF16 = jnp.bfloat16
EPS = 1e-6
LANES = 128
SUBLANES = 8
VMEM_LIMIT_BYTES = 56 * 2 ** 20
CONV_HALO = 32


def _dot(a, b):
    return jnp.dot(a, b, preferred_element_type=F32)


def _rms(x, g):
    return x * lax.rsqrt(jnp.mean(x * x, axis=-1, keepdims=True) + EPS) * g


def _split3(x):
    hi = x.astype(BF16)
    r = x - hi.astype(F32)
    mid = r.astype(BF16)
    lo = (r - mid.astype(F32)).astype(BF16)
    return hi, mid, lo


def _dot_exact_lhs01(m01, x):
    hi, mid, lo = _split3(x)
    return _dot(m01, hi) + _dot(m01, mid) + _dot(m01, lo)


def _dot_exact_rhs01(x, m01):
    hi, mid, lo = _split3(x)
    return _dot(hi, m01) + _dot(mid, m01) + _dot(lo, m01)


def _const_spec(shape):
    nd = len(shape)
    return pl.BlockSpec(shape, lambda *_: (0,) * nd, pipeline_mode=pl.Buffered(1))


def _params(*semantics):
    return pltpu.CompilerParams(dimension_semantics=semantics, vmem_limit_bytes=VMEM_LIMIT_BYTES)


def _ffn_kernel(*refs, final_norm):
    if final_norm:
        x_ref, g_ref, wg_ref, wu_ref, wd_ref, gf_ref, o_ref = refs
    else:
        x_ref, g_ref, wg_ref, wu_ref, wd_ref, o_ref = refs
    x = x_ref[...]
    h = _rms(x, g_ref[...]).astype(BF16)
    gate = _dot(h, wg_ref[...])
    up = _dot(h, wu_ref[...])
    act = (gate * jax.nn.sigmoid(gate) * up).astype(BF16)
    y = x + 0.5 * _dot(act, wd_ref[...])
    if final_norm:
        y = _rms(y, gf_ref[...])
    o_ref[...] = y


def _ffn(x2d, g, wg, wu, wd, gf=None, *, tm):
    n, d = x2d.shape
    f = wg.shape[1]
    in_specs = [pl.BlockSpec((tm, d), lambda i: (i, 0)), _const_spec((1, d)),
                _const_spec((d, f)), _const_spec((d, f)), _const_spec((f, d))]
    args = [x2d, g.reshape(1, d), wg, wu, wd]
    if gf is not None:
        in_specs.append(_const_spec((1, d)))
        args.append(gf.reshape(1, d))
    return pl.pallas_call(
        functools.partial(_ffn_kernel, final_norm=gf is not None),
        grid=(n // tm,),
        in_specs=in_specs,
        out_specs=pl.BlockSpec((tm, d), lambda i: (i, 0)),
        out_shape=jax.ShapeDtypeStruct((n, d), F32),
        compiler_params=_params("parallel"),
        name="ffn",
    )(*args)


def _inproj_kernel(x_ref, g_ref, wa_ref, wb_ref, wq_ref, wk_ref, wv_ref, wf_ref, wgc_ref, wga_ref, bf_ref,
                   glu_ref, q_ref, k_ref, v_ref, kb_ref, vb_ref, lf_ref, c_ref, gc_ref, ga_ref,
                   carry_ref, *, seg, q_scale):
    t = pl.program_id(1)
    tm = x_ref.shape[1]
    h = _rms(x_ref[0], g_ref[...]).astype(BF16)
    glu_ref[0] = _dot(h, wa_ref[...]) * jax.nn.sigmoid(_dot(h, wb_ref[...]))
    q_ref[0] = (_dot(h, wq_ref[...]) * q_scale).astype(BF16)
    k = _dot(h, wk_ref[...])
    k_ref[0] = k
    kb_ref[0] = k.astype(BF16)
    v = _dot(h, wv_ref[...])
    v_ref[0] = v
    vb_ref[0] = v.astype(BF16)
    gc_ref[0] = jax.nn.sigmoid(_dot(h, wgc_ref[...]))
    ga_ref[0] = jax.nn.sigmoid(_dot(h, wga_ref[...]))

    lf = jax.nn.log_sigmoid(_dot(h, wf_ref[...]) + bf_ref[...])
    lf_ref[0] = lf
    row = lax.broadcasted_iota(jnp.int32, (tm, tm), 0)
    col = lax.broadcasted_iota(jnp.int32, (tm, tm), 1)
    keep = col <= row
    if seg < tm:
        shift = int(math.log2(seg))
        keep = keep & (jnp.right_shift(row, shift) == jnp.right_shift(col, shift))
    tri = jnp.where(keep, 1.0, 0.0).astype(BF16)

    @pl.when(t == 0)
    def _():
        carry_ref[...] = jnp.zeros_like(carry_ref)

    c = _dot_exact_lhs01(tri, lf) + carry_ref[...]
    c_ref[0] = c
    carry_ref[...] = c_ref[0, tm - 1:tm, :]


def _inproj(x3, g, w, b_forget, *, tm, seg, q_scale):
    nb, t, d = x3.shape
    nt = t // tm
    assert seg >= tm or (nt == 1 and tm % seg == 0 and seg & (seg - 1) == 0)
    wa, wb, wq, wk, wv, wf, wgc, wga = w
    cc, aw, nh = wa.shape[1], wq.shape[1], wf.shape[1]
    tile = lambda width: pl.BlockSpec((1, tm, width), lambda b, i: (b, i, 0))
    shape = lambda width, dt: jax.ShapeDtypeStruct((nb, t, width), dt)
    in_specs = [tile(d), _const_spec((1, d))] + [_const_spec(m.shape) for m in w] + [_const_spec((1, nh))]
    out_specs = [tile(cc), tile(aw), tile(aw), tile(aw), tile(aw), tile(aw), tile(nh), tile(nh), tile(d), tile(d)]
    out_shape = [shape(cc, F32), shape(aw, BF16), shape(aw, F32), shape(aw, F32), shape(aw, BF16),
                 shape(aw, BF16), shape(nh, F32), shape(nh, F32), shape(d, F32), shape(d, F32)]
    return pl.pallas_call(
        functools.partial(_inproj_kernel, seg=seg, q_scale=q_scale),
        grid=(nb, nt),
        in_specs=in_specs,
        out_specs=out_specs,
        out_shape=out_shape,
        scratch_shapes=[pltpu.VMEM((1, nh), F32)],
        compiler_params=_params("parallel", "arbitrary"),
        name="inproj",
    )(x3, g.reshape(1, d), *w, b_forget.reshape(1, nh))


def _conv_kernel(x_ref, halo_ref, hist_ref, w_ref, b_ref, g_ref, beta_ref, wp_ref, o_ref,
                 buf_ref, sh_ref, y_ref, *, rows):
    i = pl.program_id(1)
    tt, ch = x_ref.shape[1], x_ref.shape[2]
    taps = w_ref.shape[0]
    buf_ref[0:CONV_HALO, :] = jnp.where(i == 0, hist_ref[0], halo_ref[0])
    buf_ref[CONV_HALO:CONV_HALO + tt, :] = x_ref[0]
    first = CONV_HALO - (taps - 1)

    for cb in range(ch // LANES):
        lanes = slice(cb * LANES, (cb + 1) * LANES)
        for r in range(SUBLANES):
            span = CONV_HALO + tt - (SUBLANES if r else 0)
            sh_ref[r, 0:span, :] = buf_ref[r:r + span, lanes]

        def body(rb, carry, lanes=lanes):
            r0 = pl.multiple_of(rb * rows, rows)
            acc = jnp.broadcast_to(b_ref[:, lanes], (rows, LANES))
            for j in range(taps):
                shift, base = (first + j) % SUBLANES, (first + j) // SUBLANES * SUBLANES
                acc = acc + sh_ref[shift, pl.ds(r0 + base, rows), :] * w_ref[j:j + 1, lanes]
            y_ref[pl.ds(r0, rows), lanes] = acc
            return carry

        lax.fori_loop(0, tt // rows, body, 0)

    y = y_ref[...]
    mu = jnp.mean(y, axis=-1, keepdims=True)
    dev = y - mu
    var = jnp.mean(dev * dev, axis=-1, keepdims=True)
    yn = dev * lax.rsqrt(var + EPS) * g_ref[...] + beta_ref[...]
    act = (yn * jax.nn.sigmoid(yn)).astype(BF16)
    o_ref[0] = _dot(act, wp_ref[...])


def _conv(glu, hist, w_dw, b_dw, ln_g, ln_b, wp, *, tt):
    nb, t, ch = glu.shape
    nt = t // tt
    taps = w_dw.shape[0]
    assert taps - 1 <= CONV_HALO and tt % SUBLANES == 0
    if nt > 1:
        assert tt % CONV_HALO == 0
        halo_src = glu
        halo_map = lambda b, i: (b, jnp.maximum(i * (tt // CONV_HALO) - 1, 0), 0)
    else:
        halo_src = hist
        halo_map = lambda b, i: (b, 0, 0)
    rows = min(tt, 32)
    return pl.pallas_call(
        functools.partial(_conv_kernel, rows=rows),
        grid=(nb, nt),
        in_specs=[pl.BlockSpec((1, tt, ch), lambda b, i: (b, i, 0)),
                  pl.BlockSpec((1, CONV_HALO, ch), halo_map),
                  pl.BlockSpec((1, CONV_HALO, ch), lambda b, i: (b, 0, 0)),
                  _const_spec((taps, ch)), _const_spec((1, ch)), _const_spec((1, ch)), _const_spec((1, ch)),
                  _const_spec(wp.shape)],
        out_specs=pl.BlockSpec((1, tt, wp.shape[1]), lambda b, i: (b, i, 0)),
        out_shape=jax.ShapeDtypeStruct((nb, t, wp.shape[1]), F32),
        scratch_shapes=[pltpu.VMEM((CONV_HALO + tt, ch), F32), pltpu.VMEM((SUBLANES, CONV_HALO + tt, LANES), F32),
                        pltpu.VMEM((tt, ch), F32)],
        compiler_params=_params("parallel", "parallel"),
        name="conv",
    )(glu, halo_src, hist, w_dw, b_dw.reshape(1, ch), ln_g.reshape(1, ch), ln_b.reshape(1, ch), wp)


def _attn_kernel(q_ref, k_ref, v_ref, cq_ref, ck_ref, o_ref, *, tq, head_dim):
    qi = pl.program_id(2)
    q2 = q_ref[0]
    lane = lax.broadcasted_iota(jnp.int32, (tq, LANES), 1)
    in_head = [lane < head_dim, lane >= head_dim]
    qm = [jnp.where(sel, q2, jnp.zeros_like(q2)) for sel in in_head]
    cq = [cq_ref[0, 0, :, a:a + 1] for a in range(2)]
    row = lax.broadcasted_iota(jnp.int32, (tq, tq), 0)
    col = lax.broadcasted_iota(jnp.int32, (tq, tq), 1)

    def step(j, carry, masked):
        start = pl.multiple_of(j * tq, tq)
        ks = k_ref[0, pl.ds(start, tq), :]
        vs = v_ref[0, pl.ds(start, tq), :]
        out = []
        for a in range(2):
            m, l, acc = carry[a]
            t = lax.dot_general(qm[a], ks, (((1,), (1,)), ((), ())), preferred_element_type=F32)
            t = t - ck_ref[0, 0, a:a + 1, pl.ds(start, tq)]
            if masked:
                t = jnp.where(col <= row, t, -jnp.inf)
            m_new = jnp.maximum(m, jnp.max(t, axis=-1, keepdims=True) + cq[a])
            p = jnp.exp(t - (m_new - cq[a]))
            alpha = jnp.exp(m - m_new)
            l = alpha * l + jnp.sum(p, axis=-1, keepdims=True)
            acc = alpha * acc + _dot(p.astype(BF16), vs)
            out.append((m_new, l, acc))
        return tuple(out)

    init = tuple((jnp.full((tq, 1), -jnp.inf, F32), jnp.zeros((tq, 1), F32), jnp.zeros((tq, LANES), F32))
                 for _ in range(2))
    carry = lax.fori_loop(0, qi, lambda j, c: step(j, c, False), init)
    (_, l0, acc0), (_, l1, acc1) = step(qi, carry, True)
    o_ref[0] = jnp.where(in_head[0], acc0 / l0, acc1 / l1).astype(o_ref.dtype)


def _attn(q, k, v, c, *, tq, head_dim):
    nb, t, width = q.shape
    nh = width // head_dim
    per = LANES // head_dim
    assert per == 2 and nh % per == 0 and t % tq == 0
    npair = nh // per
    cq = c.reshape(nb, t, npair, per).transpose(0, 2, 1, 3)
    ck = c.reshape(nb, t, npair, per).transpose(0, 2, 3, 1)
    return pl.pallas_call(
        functools.partial(_attn_kernel, tq=tq, head_dim=head_dim),
        grid=(nb, npair, t // tq),
        in_specs=[pl.BlockSpec((1, tq, LANES), lambda b, hp, i: (b, i, hp)),
                  pl.BlockSpec((1, t, LANES), lambda b, hp, i: (b, 0, hp)),
                  pl.BlockSpec((1, t, LANES), lambda b, hp, i: (b, 0, hp)),
                  pl.BlockSpec((1, 1, tq, per), lambda b, hp, i: (b, hp, i, 0)),
                  pl.BlockSpec((1, 1, per, t), lambda b, hp, i: (b, hp, 0, 0))],
        out_specs=pl.BlockSpec((1, tq, LANES), lambda b, hp, i: (b, i, hp)),
        out_shape=jax.ShapeDtypeStruct((nb, t, width), BF16),
        compiler_params=_params("parallel", "parallel", "arbitrary"),
        name="attn",
    )(q, k, v, cq, ck)


def _decode_kernel(pt_ref, qbd_ref, ccol_ref, kn_ref, vn_ref, cnk_ref, *rest, pages_per_step, n_new):
    pp = pages_per_step
    k_refs, v_refs, lf_refs = rest[:pp], rest[pp:2 * pp], rest[2 * pp:3 * pp]
    o_ref = rest[3 * pp]
    m_ref, l_ref, acc_ref, carry_ref = rest[3 * pp + 1:]
    i = pl.program_id(1)
    page, width = k_refs[0].shape[1], k_refs[0].shape[2]
    ncol = qbd_ref.shape[2]
    nh = lf_refs[0].shape[2]
    nq = ncol // nh
    pair_rows = 2 * nq
    qbd = qbd_ref[0]
    ccol = ccol_ref[0]
    sub = lax.broadcasted_iota(jnp.int32, (page, ncol), 0)
    lan = lax.broadcasted_iota(jnp.int32, (page, ncol), 1)

    def to_col(row_vec):
        return jnp.broadcast_to(row_vec, (ncol, ncol)).T[:, 0:1]

    def update(logits_list, v_list):
        m_old = m_ref[...]
        m_new = m_old
        for lg in logits_list:
            m_new = jnp.maximum(m_new, jnp.max(lg, axis=0, keepdims=True))
        alpha = jnp.exp(m_old - m_new)
        ps = [jnp.exp(lg - m_new) for lg in logits_list]
        l_new = alpha * l_ref[...]
        for p in ps:
            l_new = l_new + jnp.sum(p, axis=0, keepdims=True)
        l_ref[...] = l_new
        m_ref[...] = m_new
        pt = [p.T.astype(BF16) for p in ps]
        pt = pt[0] if len(pt) == 1 else jnp.concatenate(pt, axis=1)
        vcat = v_list[0] if len(v_list) == 1 else jnp.concatenate(v_list, axis=0)
        alpha_col = to_col(alpha)
        for j in range(ncol // pair_rows):
            rws = slice(j * pair_rows, (j + 1) * pair_rows)
            pv = _dot(pt[rws, :], vcat[:, j * LANES:(j + 1) * LANES])
            acc_ref[rws, :] = alpha_col[rws, :] * acc_ref[rws, :] + pv

    @pl.when(i == 0)
    def _():
        m_ref[...] = jnp.full_like(m_ref, -jnp.inf)
        l_ref[...] = jnp.zeros_like(l_ref)
        acc_ref[...] = jnp.zeros_like(acc_ref)
        carry_ref[...] = jnp.zeros_like(carry_ref)
        pad = jnp.zeros((page - n_new, width), F32)
        kn = jnp.concatenate([kn_ref[0], pad], axis=0).astype(BF16)
        vn = jnp.concatenate([vn_ref[0], pad], axis=0).astype(BF16)
        cnk = jnp.concatenate([cnk_ref[0], jnp.zeros((page - n_new, ncol), F32)], axis=0)
        s = _dot(kn, qbd)
        qpos = jnp.bitwise_and(lan, nq - 1)
        update([jnp.where(sub <= qpos, s + ccol - cnk, -jnp.inf)], [vn])

    head_of_col = jnp.right_shift(lax.broadcasted_iota(jnp.int32, (nh, ncol), 1), int(math.log2(nq)))
    expand = jnp.where(head_of_col == lax.broadcasted_iota(jnp.int32, (nh, ncol), 0), 1.0, 0.0).astype(BF16)
    later = jnp.where(lax.broadcasted_iota(jnp.int32, (page, page), 1)
                      > lax.broadcasted_iota(jnp.int32, (page, page), 0), 1.0, 0.0).astype(BF16)
    carry = carry_ref[...]
    logits_list, v_list = [], []
    for j in range(pp):
        s = _dot(k_refs[j][0].astype(BF16), qbd)
        lfe = _dot_exact_rhs01(lf_refs[j][0], expand)
        r = _dot_exact_lhs01(later, lfe) + carry
        carry = carry + jnp.sum(lfe, axis=0, keepdims=True)
        logits_list.append(s + ccol + r)
        v_list.append(v_refs[j][0].astype(BF16))
    carry_ref[...] = carry
    update(logits_list, v_list)

    @pl.when(i == pl.num_programs(1) - 1)
    def _():
        head_of_row = jnp.right_shift(lax.broadcasted_iota(jnp.int32, (ncol, LANES), 0), int(math.log2(nq)))
        half_of_lane = jnp.right_shift(lax.broadcasted_iota(jnp.int32, (ncol, LANES), 1), int(math.log2(LANES // 2)))
        own_half = jnp.bitwise_and(head_of_row, 1) == half_of_lane
        o_ref[0] = jnp.where(own_half, acc_ref[...] / to_col(l_ref[...]), 0.0)


def _decode_attn(q, k_new, v_new, c_new, cache_k, cache_v, cache_logf, page_table, *, head_dim, pages_per_step):
    db, s, width = q.shape
    nh = width // head_dim
    n_pages = page_table.shape[1]
    page = cache_k.shape[1]
    ncol = nh * s
    pp = pages_per_step
    assert ncol == LANES and page == LANES and n_pages % pp == 0 and LANES // head_dim == 2
    assert s & (s - 1) == 0 and s <= page
    qt = q.reshape(db, s, nh, head_dim).transpose(0, 2, 3, 1)
    qbd = (qt[:, :, :, None, :] * jnp.eye(nh, dtype=q.dtype)[None, :, None, :, None]).reshape(db, width, ncol)
    ccol = c_new.transpose(0, 2, 1).reshape(db, 1, ncol)
    cnk = jnp.repeat(c_new, s, axis=-1)
    pt_flat = page_table.reshape(-1)

    def page_map(j):
        return lambda b, i, pt: (pt[b * n_pages + n_pages - 1 - (i * pp + j)], 0, 0)

    per_seq = lambda shape: pl.BlockSpec((1,) + shape, lambda b, i, pt: (b, 0, 0))
    in_specs = ([per_seq((width, ncol)), per_seq((1, ncol)), per_seq((s, width)), per_seq((s, width)),
                 per_seq((s, ncol))]
                + [pl.BlockSpec((1, page, width), page_map(j)) for j in range(pp)]
                + [pl.BlockSpec((1, page, width), page_map(j)) for j in range(pp)]
                + [pl.BlockSpec((1, page, nh), page_map(j)) for j in range(pp)])
    out = pl.pallas_call(
        functools.partial(_decode_kernel, pages_per_step=pp, n_new=s),
        grid_spec=pltpu.PrefetchScalarGridSpec(
            num_scalar_prefetch=1,
            grid=(db, n_pages // pp),
            in_specs=in_specs,
            out_specs=pl.BlockSpec((1, ncol, LANES), lambda b, i, pt: (b, 0, 0)),
            scratch_shapes=[pltpu.VMEM((1, ncol), F32), pltpu.VMEM((1, ncol), F32),
                            pltpu.VMEM((ncol, LANES), F32), pltpu.VMEM((1, ncol), F32)]),
        out_shape=jax.ShapeDtypeStruct((db, ncol, LANES), F32),
        compiler_params=_params("parallel", "arbitrary"),
        name="decode",
    )(pt_flat, qbd, ccol, k_new, v_new, cnk,
      *([cache_k] * pp), *([cache_v] * pp), *([cache_logf] * pp))
    o = out.reshape(db, nh // 2, 2, s, 2, head_dim)
    o = jnp.stack([o[:, :, 0, :, 0, :], o[:, :, 1, :, 1, :]], axis=2)
    return o.transpose(0, 3, 1, 2, 4).reshape(db, s, width).astype(BF16)


def _merge_kernel(x_ref, conv_ref, attn_ref, gc_ref, ga_ref, wpa_ref, wo_ref, o_ref):
    attn_y = _dot(attn_ref[...], wpa_ref[...])
    m = gc_ref[...] * conv_ref[...] + ga_ref[...] * attn_y
    o_ref[...] = x_ref[...] + _dot(m.astype(BF16), wo_ref[...])


def _merge(x2d, conv_y, attn, gc, ga, wpa, wo, *, tm):
    n, d = x2d.shape
    aw = attn.shape[1]
    tile = lambda width: pl.BlockSpec((tm, width), lambda i: (i, 0))
    return pl.pallas_call(
        _merge_kernel,
        grid=(n // tm,),
        in_specs=[tile(d), tile(d), tile(aw), tile(d), tile(d), _const_spec(wpa.shape), _const_spec(wo.shape)],
        out_specs=tile(d),
        out_shape=jax.ShapeDtypeStruct((n, d), F32),
        compiler_params=_params("parallel"),
        name="merge",
    )(x2d, conv_y, attn, gc, ga, wpa, wo)


def _token_tile(n, want):
    tm = min(n, want)
    assert n % tm == 0
    return tm


def kernel(x_prompt, x_sample, cache_k, cache_v, cache_logf, state_conv, page_table, norm_ffn1, w_ffn1_gate, w_ffn1_up, w_ffn1_down, norm_mix, w_in, b_forget, w_dw, b_dw, ln_conv_g, ln_conv_b, w_proj_conv, w_proj_attn, w_out, norm_ffn2, w_ffn2_gate, w_ffn2_up, w_ffn2_down, norm_final):
    depth = norm_ffn1.shape[0]
    nb, t, d = x_prompt.shape
    db, s, _ = x_sample.shape
    nh, head_dim = cache_k.shape[3], cache_k.shape[4]
    aw = nh * head_dim
    cc = w_dw.shape[2]
    taps = w_dw.shape[1]
    n_pool, page = cache_k.shape[1], cache_k.shape[2]
    q_scale = 1.0 / math.sqrt(head_dim)
    assert q_scale == 2.0 ** round(math.log2(q_scale))

    xp, xs = x_prompt, x_sample.reshape(1, db * s, d)
    outs = [[] for _ in range(8)]
    for l in range(depth):
        bf = lambda a: a[l].astype(BF16)
        w1 = (bf(w_ffn1_gate), bf(w_ffn1_up), bf(w_ffn1_down))
        w2 = (bf(w_ffn2_gate), bf(w_ffn2_up), bf(w_ffn2_down))
        wi = w_in[l].astype(BF16)
        edges = [0, cc, 2 * cc, 2 * cc + aw, 2 * cc + 2 * aw, 2 * cc + 3 * aw, 2 * cc + 3 * aw + nh,
                 2 * cc + 3 * aw + nh + d, 2 * cc + 3 * aw + nh + 2 * d]
        w_split = tuple(wi[:, a:b] for a, b in zip(edges[:-1], edges[1:]))
        wpc, wpa, wo = bf(w_proj_conv), bf(w_proj_attn), bf(w_out)
        last = l == depth - 1

        def half_layer(x3, seg, tm_ffn, tm_in):
            b3, t3, _ = x3.shape
            x1 = _ffn(x3.reshape(b3 * t3, d), norm_ffn1[l], *w1, tm=tm_ffn).reshape(b3, t3, d)
            return x1, _inproj(x1, norm_mix[l], w_split, b_forget[l], tm=tm_in, seg=seg, q_scale=q_scale)

        def finish(x1, conv_y, attn, gc, ga, tm):
            n = x1.shape[0] * x1.shape[1]
            x2 = _merge(x1.reshape(n, d), conv_y.reshape(n, d), attn.reshape(n, aw), gc.reshape(n, d),
                        ga.reshape(n, d), wpa, wo, tm=tm)
            return _ffn(x2, norm_ffn2[l], *w2, norm_final if last else None, tm=tm)

        tm_p = _token_tile(t, 256)
        x1, (glu, q, k, v, kb, vb, lf, c, gc, ga) = half_layer(xp, t, _token_tile(nb * t, 512), tm_p)
        hist0 = jnp.zeros((nb, CONV_HALO, cc), F32)
        conv_y = _conv(glu, hist0, w_dw[l], b_dw[l], ln_conv_g[l], ln_conv_b[l], wpc, tt=_token_tile(t, 512))
        attn = _attn(q, kb, vb, c, tq=_token_tile(t, 512), head_dim=head_dim)
        xp = finish(x1, conv_y, attn, gc, ga, _token_tile(nb * t, 256)).reshape(nb, t, d)
        if last:
            y_prompt = xp
        outs[0].append(k.reshape(nb, t, nh, head_dim))
        outs[1].append(v.reshape(nb, t, nh, head_dim))
        outs[2].append(lf)
        outs[3].append(glu[:, t - (taps - 1):])

        n_s = db * s
        x1, (glu, q, k, v, kb, vb, lf, c, gc, ga) = half_layer(xs, s, n_s, n_s)
        glu = glu.reshape(db, s, cc)
        state = state_conv[l].astype(F32)
        hist = jnp.pad(state, ((0, 0), (CONV_HALO - (taps - 1), 0), (0, 0)))
        conv_y = _conv(glu, hist, w_dw[l], b_dw[l], ln_conv_g[l], ln_conv_b[l], wpc, tt=s)
        attn = _decode_attn(q.reshape(db, s, aw), k.reshape(db, s, aw), v.reshape(db, s, aw),
                            c.reshape(db, s, nh), cache_k[l].reshape(n_pool, page, aw),
                            cache_v[l].reshape(n_pool, page, aw), cache_logf[l], page_table,
                            head_dim=head_dim, pages_per_step=8)
        xs = finish(x1, conv_y, attn, gc, ga, n_s).reshape(1, n_s, d)
        if last:
            y_sample = xs.reshape(db, s, d)
        outs[4].append(k.reshape(db, s, nh, head_dim))
        outs[5].append(v.reshape(db, s, nh, head_dim))
        outs[6].append(lf.reshape(db, s, nh))
        outs[7].append(jnp.concatenate([state, glu], axis=1)[:, s:])

    stacked = [jnp.stack(o) for o in outs]
    return (y_prompt, y_sample, *stacked)
```

```python
import functools
import math

import jax
import jax.numpy as jnp
from jax import lax
from jax.experimental import pallas as pl
from jax.experimental.pallas import tpu as pltpu

F32 = jnp.float32
BF16 = jnp.bfloat16
EPS = 1e-6
LANES = 128
SUBLANES = 8
BF16_ROWS = 16
VMEM_LIMIT_BYTES = 56 * 2 ** 20
CONV_HALO = 32


def _dot(a, b):
    return jnp.dot(a, b, preferred_element_type=F32)


def _rms(x, g):
    return x * lax.rsqrt(jnp.mean(x * x, axis=-1, keepdims=True) + EPS) * g


def _split3(x):
    hi = x.astype(BF16)
    r = x - hi.astype(F32)
    mid = r.astype(BF16)
    lo = (r - mid.astype(F32)).astype(BF16)
    return hi, mid, lo


def _dot_exact_lhs01(m01, x):
    hi, mid, lo = _split3(x)
    return _dot(m01, hi) + _dot(m01, mid) + _dot(m01, lo)


def _dot_exact_rhs01(x, m01):
    hi, mid, lo = _split3(x)
    return _dot(hi, m01) + _dot(mid, m01) + _dot(lo, m01)


def _const_spec(shape):
    nd = len(shape)
    return pl.BlockSpec(shape, lambda *_: (0,) * nd, pipeline_mode=pl.Buffered(1))


def _params(*semantics):
    return pltpu.CompilerParams(dimension_semantics=semantics, vmem_limit_bytes=VMEM_LIMIT_BYTES)


def _ffn_kernel(*refs, final_norm):
    if final_norm:
        x_ref, g_ref, wg_ref, wu_ref, wd_ref, gf_ref, o_ref = refs
    else:
        x_ref, g_ref, wg_ref, wu_ref, wd_ref, o_ref = refs
    x = x_ref[...]
    h = _rms(x, g_ref[...]).astype(BF16)
    gate = _dot(h, wg_ref[...])
    up = _dot(h, wu_ref[...])
    act = (gate * jax.nn.sigmoid(gate) * up).astype(BF16)
    y = x + 0.5 * _dot(act, wd_ref[...])
    if final_norm:
        y = _rms(y, gf_ref[...])
    o_ref[...] = y


def _ffn(x2d, g, wg, wu, wd, gf=None, *, tm):
    n, d = x2d.shape
    f = wg.shape[1]
    in_specs = [pl.BlockSpec((tm, d), lambda i: (i, 0)), _const_spec((1, d)),
                _const_spec((d, f)), _const_spec((d, f)), _const_spec((f, d))]
    args = [x2d, g.reshape(1, d), wg, wu, wd]
    if gf is not None:
        in_specs.append(_const_spec((1, d)))
        args.append(gf.reshape(1, d))
    return pl.pallas_call(
        functools.partial(_ffn_kernel, final_norm=gf is not None),
        grid=(n // tm,),
        in_specs=in_specs,
        out_specs=pl.BlockSpec((tm, d), lambda i: (i, 0)),
        out_shape=jax.ShapeDtypeStruct((n, d), F32),
        compiler_params=_params("parallel"),
        name="ffn",
    )(*args)


LOG2_E = math.log2(math.e)
BIAS_PIECES = 3
KEY_ONES_LANE = 2 * BIAS_PIECES


def _dot_nt(a, b):
    return lax.dot_general(a, b, (((1,), (1,)), ((), ())), preferred_element_type=F32)


def _inproj_kernel(*refs, seg, q_scale, prompt):
    if prompt:
        (x_ref, g_ref, wa_ref, wb_ref, wqt_ref, wk_ref, wv_ref, wvt_ref, wf_ref, wgc_ref, wga_ref, bf_ref,
         glu_ref, qt_ref, k_ref, v_ref, kb_ref, kx_ref, vt_ref, lf_ref, c_ref, gc_ref, ga_ref, carry_ref) = refs
    else:
        (x_ref, g_ref, wa_ref, wb_ref, wq_ref, wk_ref, wv_ref, wf_ref, wgc_ref, wga_ref, bf_ref,
         glu_ref, q_ref, k_ref, v_ref, lf_ref, c_ref, gc_ref, ga_ref, carry_ref) = refs
    t = pl.program_id(1)
    tm = x_ref.shape[1]
    h = _rms(x_ref[0], g_ref[...]).astype(BF16)
    glu_ref[0] = _dot(h, wa_ref[...]) * jax.nn.sigmoid(_dot(h, wb_ref[...]))
    k = _dot(h, wk_ref[...])
    k_ref[0] = k
    v_ref[0] = _dot(h, wv_ref[...])
    if prompt:
        qt_ref[0] = (_dot_nt(wqt_ref[...], h) * (q_scale * LOG2_E)).astype(BF16)
        vt_ref[0] = _dot_nt(wvt_ref[...], h).astype(BF16)
        kb_ref[0] = k.astype(BF16)
    else:
        q_ref[0] = (_dot(h, wq_ref[...]) * q_scale).astype(BF16)
    gc_ref[0] = jax.nn.sigmoid(_dot(h, wgc_ref[...]))
    ga_ref[0] = jax.nn.sigmoid(_dot(h, wga_ref[...]))

    lf = jax.nn.log_sigmoid(_dot(h, wf_ref[...]) + bf_ref[...])
    lf_ref[0] = lf
    row = lax.broadcasted_iota(jnp.int32, (tm, tm), 0)
    col = lax.broadcasted_iota(jnp.int32, (tm, tm), 1)
    keep = col <= row
    if seg < tm:
        shift = int(math.log2(seg))
        keep = keep & (jnp.right_shift(row, shift) == jnp.right_shift(col, shift))
    tri = jnp.where(keep, 1.0, 0.0).astype(BF16)

    @pl.when(t == 0)
    def _():
        carry_ref[...] = jnp.zeros_like(carry_ref)

    c = _dot_exact_lhs01(tri, lf) + carry_ref[...]
    c_ref[0] = c
    carry_ref[...] = c_ref[0, tm - 1:tm, :]

    if prompt:
        nh, width = lf.shape[1], kx_ref.shape[2]
        head = lax.broadcasted_iota(jnp.int32, (nh, width), 0)
        lane = lax.broadcasted_iota(jnp.int32, (nh, width), 1)
        in_pair = jnp.right_shift(lane, int(math.log2(LANES))) == jnp.right_shift(head, 1)
        slot = jnp.bitwise_and(lane, LANES - 1) - jnp.bitwise_and(head, 1) * BIAS_PIECES
        ext = None
        for p, piece in enumerate(_split3(c * LOG2_E)):
            place = jnp.where(in_pair & (slot == p), -1.0, 0.0).astype(BF16)
            term = _dot(piece, place)
            ext = term if ext is None else ext + term
        within = jnp.bitwise_and(lax.broadcasted_iota(jnp.int32, (1, width), 1), LANES - 1)
        ones = jnp.where((within >= KEY_ONES_LANE) & (within < KEY_ONES_LANE + BIAS_PIECES), 1.0, 0.0)
        kx_ref[0] = (ext + ones).astype(BF16)


def _inproj(x3, g, w, b_forget, *, tm, seg, q_scale, prompt):
    nb, t, d = x3.shape
    nt = t // tm
    assert seg >= tm or (nt == 1 and tm % seg == 0 and seg & (seg - 1) == 0)
    wa, wb, wq, wk, wv, wf, wgc, wga = w
    cc, aw, nh = wa.shape[1], wq.shape[1], wf.shape[1]
    tile = lambda width: pl.BlockSpec((1, tm, width), lambda b, i: (b, i, 0))
    tile_t = lambda width: pl.BlockSpec((1, width, tm), lambda b, i: (b, 0, i))
    shape = lambda width, dt: jax.ShapeDtypeStruct((nb, t, width), dt)
    shape_t = lambda width, dt: jax.ShapeDtypeStruct((nb, width, t), dt)
    if prompt:
        assert 2 * (aw // nh) == LANES and KEY_ONES_LANE + BIAS_PIECES <= LANES
        weights = (wa, wb, wq.T, wk, wv, wv.T, wf, wgc, wga)
        out_specs = [tile(cc), tile_t(aw), tile(aw), tile(aw), tile(aw), tile(aw), tile_t(aw),
                     tile(nh), tile(nh), tile(d), tile(d)]
        out_shape = [shape(cc, F32), shape_t(aw, BF16), shape(aw, F32), shape(aw, F32), shape(aw, BF16),
                     shape(aw, BF16), shape_t(aw, BF16), shape(nh, F32), shape(nh, F32), shape(d, F32), shape(d, F32)]
    else:
        weights = w
        out_specs = [tile(cc), tile(aw), tile(aw), tile(aw), tile(nh), tile(nh), tile(d), tile(d)]
        out_shape = [shape(cc, F32), shape(aw, BF16), shape(aw, F32), shape(aw, F32),
                     shape(nh, F32), shape(nh, F32), shape(d, F32), shape(d, F32)]
    in_specs = [tile(d), _const_spec((1, d))] + [_const_spec(m.shape) for m in weights] + [_const_spec((1, nh))]
    return pl.pallas_call(
        functools.partial(_inproj_kernel, seg=seg, q_scale=q_scale, prompt=prompt),
        grid=(nb, nt),
        in_specs=in_specs,
        out_specs=out_specs,
        out_shape=out_shape,
        scratch_shapes=[pltpu.VMEM((1, nh), F32)],
        compiler_params=_params("parallel", "arbitrary"),
        name="inproj",
    )(x3, g.reshape(1, d), *weights, b_forget.reshape(1, nh))


def _conv_kernel(x_ref, halo_ref, hist_ref, w_ref, b_ref, g_ref, beta_ref, wp_ref, o_ref,
                 buf_ref, sh_ref, y_ref, *, rows):
    i = pl.program_id(1)
    tt, ch = x_ref.shape[1], x_ref.shape[2]
    taps = w_ref.shape[0]
    buf_ref[0:CONV_HALO, :] = jnp.where(i == 0, hist_ref[0], halo_ref[0])
    buf_ref[CONV_HALO:CONV_HALO + tt, :] = x_ref[0]
    first = CONV_HALO - (taps - 1)

    for cb in range(ch // LANES):
        lanes = slice(cb * LANES, (cb + 1) * LANES)
        for r in range(SUBLANES):
            span = CONV_HALO + tt - (SUBLANES if r else 0)
            sh_ref[r, 0:span, :] = buf_ref[r:r + span, lanes]

        def body(rb, carry, lanes=lanes):
            r0 = pl.multiple_of(rb * rows, rows)
            acc = jnp.broadcast_to(b_ref[:, lanes], (rows, LANES))
            for j in range(taps):
                shift, base = (first + j) % SUBLANES, (first + j) // SUBLANES * SUBLANES
                acc = acc + sh_ref[shift, pl.ds(r0 + base, rows), :] * w_ref[j:j + 1, lanes]
            y_ref[pl.ds(r0, rows), lanes] = acc
            return carry

        lax.fori_loop(0, tt // rows, body, 0)

    y = y_ref[...]
    mu = jnp.mean(y, axis=-1, keepdims=True)
    dev = y - mu
    var = jnp.mean(dev * dev, axis=-1, keepdims=True)
    yn = dev * lax.rsqrt(var + EPS) * g_ref[...] + beta_ref[...]
    act = (yn * jax.nn.sigmoid(yn)).astype(BF16)
    o_ref[0] = _dot(act, wp_ref[...])


def _conv(glu, hist, w_dw, b_dw, ln_g, ln_b, wp, *, tt):
    nb, t, ch = glu.shape
    nt = t // tt
    taps = w_dw.shape[0]
    assert taps - 1 <= CONV_HALO and tt % SUBLANES == 0
    if nt > 1:
        assert tt % CONV_HALO == 0
        halo_src = glu
        halo_map = lambda b, i: (b, jnp.maximum(i * (tt // CONV_HALO) - 1, 0), 0)
    else:
        halo_src = hist
        halo_map = lambda b, i: (b, 0, 0)
    rows = min(tt, 32)
    return pl.pallas_call(
        functools.partial(_conv_kernel, rows=rows),
        grid=(nb, nt),
        in_specs=[pl.BlockSpec((1, tt, ch), lambda b, i: (b, i, 0)),
                  pl.BlockSpec((1, CONV_HALO, ch), halo_map),
                  pl.BlockSpec((1, CONV_HALO, ch), lambda b, i: (b, 0, 0)),
                  _const_spec((taps, ch)), _const_spec((1, ch)), _const_spec((1, ch)), _const_spec((1, ch)),
                  _const_spec(wp.shape)],
        out_specs=pl.BlockSpec((1, tt, wp.shape[1]), lambda b, i: (b, i, 0)),
        out_shape=jax.ShapeDtypeStruct((nb, t, wp.shape[1]), F32),
        scratch_shapes=[pltpu.VMEM((CONV_HALO + tt, ch), F32), pltpu.VMEM((SUBLANES, CONV_HALO + tt, LANES), F32),
                        pltpu.VMEM((tt, ch), F32)],
        compiler_params=_params("parallel", "parallel"),
        name="conv",
    )(glu, halo_src, hist, w_dw, b_dw.reshape(1, ch), ln_g.reshape(1, ch), ln_b.reshape(1, ch), wp)


def _attn_kernel(qt_ref, k_ref, kx_ref, vt_ref, cq_ref, o_ref, sa_ref, sb_ref, *, tq, head_dim):
    qi = pl.program_id(2)
    qt2 = qt_ref[0].astype(F32)
    sub = lax.broadcasted_iota(jnp.int32, (LANES, tq), 0)
    qaug = []
    for a in range(2):
        qm = jnp.where((sub >= head_dim) == bool(a), qt2, 0.0)
        ext = jnp.where((sub >= a * BIAS_PIECES) & (sub < (a + 1) * BIAS_PIECES), 1.0, 0.0)
        for p, piece in enumerate(_split3(cq_ref[0, 0, a:a + 1, :] * LOG2_E)):
            ext = jnp.where(sub == KEY_ONES_LANE + p, piece.astype(F32), ext)
        qaug.append(jnp.concatenate([qm, ext], axis=0).astype(BF16))
    tk = tq // 2
    ones_rows = jnp.ones((BF16_ROWS, tk), BF16)
    key = lax.broadcasted_iota(jnp.int32, (tk, tq), 0)
    qry = lax.broadcasted_iota(jnp.int32, (tk, tq), 1)

    def logits(j, s_ref):
        start = pl.multiple_of(j * tk, tk)
        kcat = jnp.concatenate([k_ref[0, pl.ds(start, tk), :], kx_ref[0, pl.ds(start, tk), :]], axis=1)
        for a in range(2):
            s_ref[a] = _dot(kcat, qaug[a])

    def absorb(j, s_ref, state, key_offset=None):
        start = pl.multiple_of(j * tk, tk)
        out = []
        for a in range(2):
            m, acc = state[a]
            sa = s_ref[a]
            if key_offset is not None:
                sa = jnp.where(key + key_offset <= qry, sa, -jnp.inf)
            m_new = jnp.maximum(m, jnp.max(sa, axis=0, keepdims=True))
            p = jnp.exp2(sa - m_new).astype(BF16)
            vta = jnp.concatenate([vt_ref[0, a * head_dim:(a + 1) * head_dim, pl.ds(start, tk)], ones_rows], axis=0)
            acc = jnp.exp2(m - m_new) * acc + _dot(vta, p)
            out.append((m_new, acc))
        return tuple(out)

    def body(jj, state):
        logits(2 * jj + 1, sb_ref)
        state = absorb(2 * jj, sa_ref, state)
        logits(2 * jj + 2, sa_ref)
        return absorb(2 * jj + 1, sb_ref, state)

    init = tuple((jnp.full((1, tq), -jnp.inf, F32), jnp.zeros((head_dim + BF16_ROWS, tq), F32)) for _ in range(2))
    logits(0, sa_ref)
    state = lax.fori_loop(0, qi, body, init)
    logits(2 * qi + 1, sb_ref)
    state = absorb(2 * qi, sa_ref, state, key_offset=0)
    (_, acc0), (_, acc1) = absorb(2 * qi + 1, sb_ref, state, key_offset=tk)
    heads = [acc[:head_dim] / acc[head_dim:head_dim + 1] for acc in (acc0, acc1)]
    o_ref[0] = jnp.concatenate(heads, axis=0).T.astype(o_ref.dtype)


def _attn(qt, k, kx, vt, c, *, tq, head_dim):
    nb, width, t = qt.shape
    nh = width // head_dim
    per = LANES // head_dim
    assert per == 2 and nh % per == 0 and t % tq == 0
    npair = nh // per
    cq = c.reshape(nb, t, npair, per).transpose(0, 2, 3, 1)
    return pl.pallas_call(
        functools.partial(_attn_kernel, tq=tq, head_dim=head_dim),
        grid=(nb, npair, t // tq),
        in_specs=[pl.BlockSpec((1, LANES, tq), lambda b, hp, i: (b, hp, i)),
                  pl.BlockSpec((1, t, LANES), lambda b, hp, i: (b, 0, hp)),
                  pl.BlockSpec((1, t, LANES), lambda b, hp, i: (b, 0, hp)),
                  pl.BlockSpec((1, LANES, t), lambda b, hp, i: (b, hp, 0)),
                  pl.BlockSpec((1, 1, per, tq), lambda b, hp, i: (b, hp, 0, i))],
        out_specs=pl.BlockSpec((1, tq, LANES), lambda b, hp, i: (b, i, hp)),
        out_shape=jax.ShapeDtypeStruct((nb, t, width), BF16),
        scratch_shapes=[pltpu.VMEM((per, tq // 2, tq), F32), pltpu.VMEM((per, tq // 2, tq), F32)],
        compiler_params=_params("parallel", "parallel", "arbitrary"),
        name="attn",
    )(qt, k, kx, vt, cq)


def _gather_kernel(pt_ref, *refs, pages_per_step):
    pp = pages_per_step
    k_refs, v_refs, lf_refs = refs[:pp], refs[pp:2 * pp], refs[2 * pp:3 * pp]
    ko_ref, vo_ref, lfo_ref = refs[3 * pp:]
    page, width = k_refs[0].shape[2], ko_ref.shape[2]
    for j in range(pp):
        rows = slice(j * page, (j + 1) * page)
        ko_ref[0, rows, :] = k_refs[j][0, 0].reshape(page, width).astype(BF16)
        vo_ref[0, rows, :] = v_refs[j][0, 0].reshape(page, width).astype(BF16)
        lfo_ref[0, rows, :] = lf_refs[j][0, 0]


def _gather_past(cache_k, cache_v, cache_logf, page_table, layer, *, pages_per_step):
    _, _, page, nh, head_dim = cache_k.shape
    db, n_pages = page_table.shape
    pp = pages_per_step
    width = nh * head_dim
    assert n_pages % pp == 0

    def page_map(j, nd):
        return lambda b, i, pt: (layer, pt[b * n_pages + i * pp + j]) + (0,) * nd

    in_specs = ([pl.BlockSpec((1, 1, page, nh, head_dim), page_map(j, 3)) for j in range(pp)]
                + [pl.BlockSpec((1, 1, page, nh, head_dim), page_map(j, 3)) for j in range(pp)]
                + [pl.BlockSpec((1, 1, page, nh), page_map(j, 2)) for j in range(pp)])
    rows = lambda w: pl.BlockSpec((1, pp * page, w), lambda b, i, pt: (b, i, 0))
    return pl.pallas_call(
        functools.partial(_gather_kernel, pages_per_step=pp),
        grid_spec=pltpu.PrefetchScalarGridSpec(
            num_scalar_prefetch=1,
            grid=(db, n_pages // pp),
            in_specs=in_specs,
            out_specs=[rows(width), rows(width), rows(nh)]),
        out_shape=[jax.ShapeDtypeStruct((db, n_pages * page, width), BF16),
                   jax.ShapeDtypeStruct((db, n_pages * page, width), BF16),
                   jax.ShapeDtypeStruct((db, n_pages * page, nh), F32)],
        compiler_params=_params("parallel", "parallel"),
        name="gather",
    )(page_table.reshape(-1), *([cache_k] * pp), *([cache_v] * pp), *([cache_logf] * pp))


def _decode_kernel(qbd_ref, ccol_ref, kn_ref, vn_ref, cnk_ref, kp_ref, vp_ref, lfp_ref, o_ref,
                   m_ref, l_ref, acc_ref, carry_ref, *, page, n_new):
    pp = kp_ref.shape[1] // page
    i = pl.program_id(1)
    width = kp_ref.shape[2]
    ncol = qbd_ref.shape[2]
    nh = lfp_ref.shape[2]
    nq = ncol // nh
    pair_rows = 2 * nq
    qbd = qbd_ref[0]
    ccol = ccol_ref[0]
    sub = lax.broadcasted_iota(jnp.int32, (page, ncol), 0)
    lan = lax.broadcasted_iota(jnp.int32, (page, ncol), 1)

    def to_col(row_vec):
        return jnp.broadcast_to(row_vec, (ncol, ncol)).T[:, 0:1]

    def update(logits_list, v_list):
        m_old = m_ref[...]
        m_new = m_old
        for lg in logits_list:
            m_new = jnp.maximum(m_new, jnp.max(lg, axis=0, keepdims=True))
        alpha = jnp.exp(m_old - m_new)
        ps = [jnp.exp(lg - m_new) for lg in logits_list]
        l_new = alpha * l_ref[...]
        for p in ps:
            l_new = l_new + jnp.sum(p, axis=0, keepdims=True)
        l_ref[...] = l_new
        m_ref[...] = m_new
        pt = [p.T.astype(BF16) for p in ps]
        pt = pt[0] if len(pt) == 1 else jnp.concatenate(pt, axis=1)
        vcat = v_list[0] if len(v_list) == 1 else jnp.concatenate(v_list, axis=0)
        alpha_col = to_col(alpha)
        for j in range(ncol // pair_rows):
            rws = slice(j * pair_rows, (j + 1) * pair_rows)
            pv = _dot(pt[rws, :], vcat[:, j * LANES:(j + 1) * LANES])
            acc_ref[rws, :] = alpha_col[rws, :] * acc_ref[rws, :] + pv

    @pl.when(i == 0)
    def _():
        m_ref[...] = jnp.full_like(m_ref, -jnp.inf)
        l_ref[...] = jnp.zeros_like(l_ref)
        acc_ref[...] = jnp.zeros_like(acc_ref)
        carry_ref[...] = jnp.zeros_like(carry_ref)
        pad = jnp.zeros((page - n_new, width), F32)
        kn = jnp.concatenate([kn_ref[0], pad], axis=0).astype(BF16)
        vn = jnp.concatenate([vn_ref[0], pad], axis=0).astype(BF16)
        cnk = jnp.concatenate([cnk_ref[0], jnp.zeros((page - n_new, ncol), F32)], axis=0)
        s = _dot(kn, qbd)
        qpos = jnp.bitwise_and(lan, nq - 1)
        update([jnp.where(sub <= qpos, s + ccol - cnk, -jnp.inf)], [vn])

    head_of_col = jnp.right_shift(lax.broadcasted_iota(jnp.int32, (nh, ncol), 1), int(math.log2(nq)))
    expand = jnp.where(head_of_col == lax.broadcasted_iota(jnp.int32, (nh, ncol), 0), 1.0, 0.0).astype(BF16)
    later = jnp.where(lax.broadcasted_iota(jnp.int32, (page, page), 1)
                      > lax.broadcasted_iota(jnp.int32, (page, page), 0), 1.0, 0.0).astype(BF16)
    carry = carry_ref[...]
    logits_list, v_list = [], []
    for j in reversed(range(pp)):
        rows = slice(j * page, (j + 1) * page)
        s = _dot(kp_ref[0, rows, :], qbd)
        lfe = _dot_exact_rhs01(lfp_ref[0, rows, :], expand)
        r = _dot_exact_lhs01(later, lfe) + carry
        carry = carry + jnp.sum(lfe, axis=0, keepdims=True)
        logits_list.append(s + ccol + r)
        v_list.append(vp_ref[0, rows, :])
    carry_ref[...] = carry
    update(logits_list, v_list)

    @pl.when(i == pl.num_programs(1) - 1)
    def _():
        head_of_row = jnp.right_shift(lax.broadcasted_iota(jnp.int32, (ncol, LANES), 0), int(math.log2(nq)))
        half_of_lane = jnp.right_shift(lax.broadcasted_iota(jnp.int32, (ncol, LANES), 1), int(math.log2(LANES // 2)))
        own_half = jnp.bitwise_and(head_of_row, 1) == half_of_lane
        o_ref[0] = jnp.where(own_half, acc_ref[...] / to_col(l_ref[...]), 0.0)


def _decode_attn(q, k_new, v_new, c_new, k_past, v_past, lf_past, *, head_dim, page, pages_per_step):
    db, s, width = q.shape
    nh = width // head_dim
    n_pages = k_past.shape[1] // page
    ncol = nh * s
    pp = pages_per_step
    assert ncol == LANES and page == LANES and n_pages % pp == 0 and LANES // head_dim == 2
    assert s & (s - 1) == 0 and s <= page
    qt = q.reshape(db, s, nh, head_dim).transpose(0, 2, 3, 1)
    qbd = (qt[:, :, :, None, :] * jnp.eye(nh, dtype=q.dtype)[None, :, None, :, None]).reshape(db, width, ncol)
    ccol = c_new.transpose(0, 2, 1).reshape(db, 1, ncol)
    cnk = jnp.repeat(c_new, s, axis=-1)
    n_steps = n_pages // pp
    per_seq = lambda shape: pl.BlockSpec((1,) + shape, lambda b, i: (b, 0, 0))
    past = lambda w: pl.BlockSpec((1, pp * page, w), lambda b, i: (b, n_steps - 1 - i, 0))
    out = pl.pallas_call(
        functools.partial(_decode_kernel, page=page, n_new=s),
        grid=(db, n_steps),
        in_specs=[per_seq((width, ncol)), per_seq((1, ncol)), per_seq((s, width)), per_seq((s, width)),
                  per_seq((s, ncol)), past(width), past(width), past(nh)],
        out_specs=pl.BlockSpec((1, ncol, LANES), lambda b, i: (b, 0, 0)),
        out_shape=jax.ShapeDtypeStruct((db, ncol, LANES), F32),
        scratch_shapes=[pltpu.VMEM((1, ncol), F32), pltpu.VMEM((1, ncol), F32),
                        pltpu.VMEM((ncol, LANES), F32), pltpu.VMEM((1, ncol), F32)],
        compiler_params=_params("parallel", "arbitrary"),
        name="decode",
    )(qbd, ccol, k_new, v_new, cnk, k_past, v_past, lf_past)
    o = out.reshape(db, nh // 2, 2, s, 2, head_dim)
    o = jnp.stack([o[:, :, 0, :, 0, :], o[:, :, 1, :, 1, :]], axis=2)
    return o.transpose(0, 3, 1, 2, 4).reshape(db, s, width).astype(BF16)


def _merge_kernel(x_ref, conv_ref, attn_ref, gc_ref, ga_ref, wpa_ref, wo_ref, o_ref):
    attn_y = _dot(attn_ref[...], wpa_ref[...])
    m = gc_ref[...] * conv_ref[...] + ga_ref[...] * attn_y
    o_ref[...] = x_ref[...] + _dot(m.astype(BF16), wo_ref[...])


def _merge(x2d, conv_y, attn, gc, ga, wpa, wo, *, tm):
    n, d = x2d.shape
    aw = attn.shape[1]
    tile = lambda width: pl.BlockSpec((tm, width), lambda i: (i, 0))
    return pl.pallas_call(
        _merge_kernel,
        grid=(n // tm,),
        in_specs=[tile(d), tile(d), tile(aw), tile(d), tile(d), _const_spec(wpa.shape), _const_spec(wo.shape)],
        out_specs=tile(d),
        out_shape=jax.ShapeDtypeStruct((n, d), F32),
        compiler_params=_params("parallel"),
        name="merge",
    )(x2d, conv_y, attn, gc, ga, wpa, wo)


def _token_tile(n, want):
    tm = min(n, want)
    assert n % tm == 0
    return tm


def kernel(x_prompt, x_sample, cache_k, cache_v, cache_logf, state_conv, page_table, norm_ffn1, w_ffn1_gate, w_ffn1_up, w_ffn1_down, norm_mix, w_in, b_forget, w_dw, b_dw, ln_conv_g, ln_conv_b, w_proj_conv, w_proj_attn, w_out, norm_ffn2, w_ffn2_gate, w_ffn2_up, w_ffn2_down, norm_final):
    depth = norm_ffn1.shape[0]
    nb, t, d = x_prompt.shape
    db, s, _ = x_sample.shape
    nh, head_dim = cache_k.shape[3], cache_k.shape[4]
    aw = nh * head_dim
    cc = w_dw.shape[2]
    taps = w_dw.shape[1]
    n_pool, page = cache_k.shape[1], cache_k.shape[2]
    q_scale = 1.0 / math.sqrt(head_dim)
    assert q_scale == 2.0 ** round(math.log2(q_scale))

    xp, xs = x_prompt, x_sample.reshape(1, db * s, d)
    outs = [[] for _ in range(8)]
    for l in range(depth):
        bf = lambda a: a[l].astype(BF16)
        w1 = (bf(w_ffn1_gate), bf(w_ffn1_up), bf(w_ffn1_down))
        w2 = (bf(w_ffn2_gate), bf(w_ffn2_up), bf(w_ffn2_down))
        wi = w_in[l].astype(BF16)
        edges = [0, cc, 2 * cc, 2 * cc + aw, 2 * cc + 2 * aw, 2 * cc + 3 * aw, 2 * cc + 3 * aw + nh,
                 2 * cc + 3 * aw + nh + d, 2 * cc + 3 * aw + nh + 2 * d]
        w_split = tuple(wi[:, a:b] for a, b in zip(edges[:-1], edges[1:]))
        wpc, wpa, wo = bf(w_proj_conv), bf(w_proj_attn), bf(w_out)
        last = l == depth - 1

        def half_layer(x3, seg, tm_ffn, tm_in, prompt):
            b3, t3, _ = x3.shape
            x1 = _ffn(x3.reshape(b3 * t3, d), norm_ffn1[l], *w1, tm=tm_ffn).reshape(b3, t3, d)
            return x1, _inproj(x1, norm_mix[l], w_split, b_forget[l], tm=tm_in, seg=seg, q_scale=q_scale,
                               prompt=prompt)

        def finish(x1, conv_y, attn, gc, ga, tm):
            n = x1.shape[0] * x1.shape[1]
            x2 = _merge(x1.reshape(n, d), conv_y.reshape(n, d), attn.reshape(n, aw), gc.reshape(n, d),
                        ga.reshape(n, d), wpa, wo, tm=tm)
            return _ffn(x2, norm_ffn2[l], *w2, norm_final if last else None, tm=tm)

        tm_p = _token_tile(t, 256)
        x1, (glu, qt, k, v, kb, kx, vt, lf, c, gc, ga) = half_layer(xp, t, _token_tile(nb * t, 512), tm_p, True)
        hist0 = jnp.zeros((nb, CONV_HALO, cc), F32)
        conv_y = _conv(glu, hist0, w_dw[l], b_dw[l], ln_conv_g[l], ln_conv_b[l], wpc, tt=_token_tile(t, 512))
        attn = _attn(qt, kb, kx, vt, c, tq=_token_tile(t, 512), head_dim=head_dim)
        xp = finish(x1, conv_y, attn, gc, ga, _token_tile(nb * t, 256)).reshape(nb, t, d)
        if last:
            y_prompt = xp
        outs[0].append(k.reshape(nb, t, nh, head_dim))
        outs[1].append(v.reshape(nb, t, nh, head_dim))
        outs[2].append(lf)
        outs[3].append(glu[:, t - (taps - 1):])

        n_s = db * s
        x1, (glu, q, k, v, lf, c, gc, ga) = half_layer(xs, s, n_s, n_s, False)
        glu = glu.reshape(db, s, cc)
        state = state_conv[l].astype(F32)
        hist = jnp.pad(state, ((0, 0), (CONV_HALO - (taps - 1), 0), (0, 0)))
        conv_y = _conv(glu, hist, w_dw[l], b_dw[l], ln_conv_g[l], ln_conv_b[l], wpc, tt=s)
        k_past, v_past, lf_past = _gather_past(cache_k, cache_v, cache_logf, page_table, l, pages_per_step=4)
        attn = _decode_attn(q.reshape(db, s, aw), k.reshape(db, s, aw), v.reshape(db, s, aw),
                            c.reshape(db, s, nh), k_past, v_past, lf_past,
                            head_dim=head_dim, page=page, pages_per_step=8)
        xs = finish(x1, conv_y, attn, gc, ga, n_s).reshape(1, n_s, d)
        if last:
            y_sample = xs.reshape(db, s, d)
        outs[4].append(k.reshape(db, s, nh, head_dim))
        outs[5].append(v.reshape(db, s, nh, head_dim))
        outs[6].append(lf.reshape(db, s, nh))
        outs[7].append(jnp.concatenate([state, glu], axis=1)[:, s:])

    stacked = [jnp.stack(o) for o in outs]
    return (y_prompt, y_sample, *stacked)
```

```python
import functools
import math

import jax
import jax.numpy as jnp
from jax import lax
from jax.experimental import pallas as pl
from jax.experimental.pallas import tpu as pltpu

F32 = jnp.float32
BF16 = jnp.bfloat16
EPS = 1e-6
LANES = 128
SUBLANES = 8
BF16_ROWS = 16
VMEM_LIMIT_BYTES = 56 * 2 ** 20
CONV_HALO = 32


def _dot(a, b):
    return jnp.dot(a, b, preferred_element_type=F32)


def _rms(x, g):
    return x * lax.rsqrt(jnp.mean(x * x, axis=-1, keepdims=True) + EPS) * g


def _split3(x):
    hi = x.astype(BF16)
    r = x - hi.astype(F32)
    mid = r.astype(BF16)
    lo = (r - mid.astype(F32)).astype(BF16)
    return hi, mid, lo


def _dot_exact_lhs01(m01, x):
    hi, mid, lo = _split3(x)
    return _dot(m01, hi) + _dot(m01, mid) + _dot(m01, lo)


def _dot_exact_rhs01(x, m01):
    hi, mid, lo = _split3(x)
    return _dot(hi, m01) + _dot(mid, m01) + _dot(lo, m01)


def _const_spec(shape):
    nd = len(shape)
    return pl.BlockSpec(shape, lambda *_: (0,) * nd, pipeline_mode=pl.Buffered(1))


def _params(*semantics):
    return pltpu.CompilerParams(dimension_semantics=semantics, vmem_limit_bytes=VMEM_LIMIT_BYTES)


def _ffn_kernel(*refs, final_norm):
    if final_norm:
        x_ref, g_ref, wg_ref, wu_ref, wd_ref, gf_ref, o_ref = refs
    else:
        x_ref, g_ref, wg_ref, wu_ref, wd_ref, o_ref = refs
    x = x_ref[...]
    h = _rms(x, g_ref[...]).astype(BF16)
    gate = _dot(h, wg_ref[...])
    up = _dot(h, wu_ref[...])
    act = (gate * jax.nn.sigmoid(gate) * up).astype(BF16)
    y = x + 0.5 * _dot(act, wd_ref[...])
    if final_norm:
        y = _rms(y, gf_ref[...])
    o_ref[...] = y


def _ffn(x2d, g, wg, wu, wd, gf=None, *, tm):
    n, d = x2d.shape
    f = wg.shape[1]
    in_specs = [pl.BlockSpec((tm, d), lambda i: (i, 0)), _const_spec((1, d)),
                _const_spec((d, f)), _const_spec((d, f)), _const_spec((f, d))]
    args = [x2d, g.reshape(1, d), wg, wu, wd]
    if gf is not None:
        in_specs.append(_const_spec((1, d)))
        args.append(gf.reshape(1, d))
    return pl.pallas_call(
        functools.partial(_ffn_kernel, final_norm=gf is not None),
        grid=(n // tm,),
        in_specs=in_specs,
        out_specs=pl.BlockSpec((tm, d), lambda i: (i, 0)),
        out_shape=jax.ShapeDtypeStruct((n, d), F32),
        compiler_params=_params("parallel"),
        name="ffn",
    )(*args)


LOG2_E = math.log2(math.e)
BIAS_PIECES = 3
KEY_ONES_LANE = 2 * BIAS_PIECES


def _dot_nt(a, b):
    return lax.dot_general(a, b, (((1,), (1,)), ((), ())), preferred_element_type=F32)


def _inproj_kernel(*refs, seg, q_scale, prompt):
    if prompt:
        (x_ref, g_ref, wa_ref, wb_ref, wqt_ref, wk_ref, wkt_ref, wvt_ref, wf_ref, wgc_ref, wga_ref, bf_ref,
         glu_ref, qt_ref, kt_ref, vt_ref, kb_ref, kx_ref, vtb_ref, lf_ref, c_ref, gc_ref, ga_ref, carry_ref) = refs
    else:
        (x_ref, g_ref, wa_ref, wb_ref, wq_ref, wk_ref, wv_ref, wf_ref, wgc_ref, wga_ref, bf_ref,
         glu_ref, q_ref, k_ref, v_ref, lf_ref, c_ref, gc_ref, ga_ref, carry_ref) = refs
    t = pl.program_id(1)
    tm = x_ref.shape[1]
    h = _rms(x_ref[0], g_ref[...]).astype(BF16)
    glu_ref[0] = _dot(h, wa_ref[...]) * jax.nn.sigmoid(_dot(h, wb_ref[...]))
    if prompt:
        qt_ref[0] = (_dot_nt(wqt_ref[...], h) * (q_scale * LOG2_E)).astype(BF16)
        kt_ref[0] = _dot_nt(wkt_ref[...], h)
        vt = _dot_nt(wvt_ref[...], h)
        vt_ref[0] = vt
        vtb_ref[0] = vt.astype(BF16)
        kb_ref[0] = _dot(h, wk_ref[...]).astype(BF16)
    else:
        q_ref[0] = (_dot(h, wq_ref[...]) * q_scale).astype(BF16)
        k_ref[0] = _dot(h, wk_ref[...])
        v_ref[0] = _dot(h, wv_ref[...])
    gc_ref[0] = jax.nn.sigmoid(_dot(h, wgc_ref[...]))
    ga_ref[0] = jax.nn.sigmoid(_dot(h, wga_ref[...]))

    lf = jax.nn.log_sigmoid(_dot(h, wf_ref[...]) + bf_ref[...])
    lf_ref[0] = lf
    row = lax.broadcasted_iota(jnp.int32, (tm, tm), 0)
    col = lax.broadcasted_iota(jnp.int32, (tm, tm), 1)
    keep = col <= row
    if seg < tm:
        shift = int(math.log2(seg))
        keep = keep & (jnp.right_shift(row, shift) == jnp.right_shift(col, shift))
    tri = jnp.where(keep, 1.0, 0.0).astype(BF16)

    @pl.when(t == 0)
    def _():
        carry_ref[...] = jnp.zeros_like(carry_ref)

    c = _dot_exact_lhs01(tri, lf) + carry_ref[...]
    c_ref[0] = c
    carry_ref[...] = c_ref[0, tm - 1:tm, :]

    if prompt:
        nh, width = lf.shape[1], kx_ref.shape[2]
        head = lax.broadcasted_iota(jnp.int32, (nh, width), 0)
        lane = lax.broadcasted_iota(jnp.int32, (nh, width), 1)
        in_pair = jnp.right_shift(lane, int(math.log2(LANES))) == jnp.right_shift(head, 1)
        slot = jnp.bitwise_and(lane, LANES - 1) - jnp.bitwise_and(head, 1) * BIAS_PIECES
        ext = None
        for p, piece in enumerate(_split3(c * LOG2_E)):
            place = jnp.where(in_pair & (slot == p), -1.0, 0.0).astype(BF16)
            term = _dot(piece, place)
            ext = term if ext is None else ext + term
        within = jnp.bitwise_and(lax.broadcasted_iota(jnp.int32, (1, width), 1), LANES - 1)
        ones = jnp.where((within >= KEY_ONES_LANE) & (within < KEY_ONES_LANE + BIAS_PIECES), 1.0, 0.0)
        kx_ref[0] = (ext + ones).astype(BF16)


def _inproj(x3, g, w, b_forget, *, tm, seg, q_scale, prompt):
    nb, t, d = x3.shape
    nt = t // tm
    assert seg >= tm or (nt == 1 and tm % seg == 0 and seg & (seg - 1) == 0)
    wa, wb, wq, wk, wv, wf, wgc, wga = w
    cc, aw, nh = wa.shape[1], wq.shape[1], wf.shape[1]
    tile = lambda width: pl.BlockSpec((1, tm, width), lambda b, i: (b, i, 0))
    tile_t = lambda width: pl.BlockSpec((1, width, tm), lambda b, i: (b, 0, i))
    shape = lambda width, dt: jax.ShapeDtypeStruct((nb, t, width), dt)
    shape_t = lambda width, dt: jax.ShapeDtypeStruct((nb, width, t), dt)
    if prompt:
        assert 2 * (aw // nh) == LANES and KEY_ONES_LANE + BIAS_PIECES <= LANES
        weights = (wa, wb, wq.T, wk, wk.T, wv.T, wf, wgc, wga)
        out_specs = [tile(cc), tile_t(aw), tile_t(aw), tile_t(aw), tile(aw), tile(aw), tile_t(aw),
                     tile(nh), tile(nh), tile(d), tile(d)]
        out_shape = [shape(cc, F32), shape_t(aw, BF16), shape_t(aw, F32), shape_t(aw, F32), shape(aw, BF16),
                     shape(aw, BF16), shape_t(aw, BF16), shape(nh, F32), shape(nh, F32), shape(d, F32), shape(d, F32)]
    else:
        weights = w
        out_specs = [tile(cc), tile(aw), tile(aw), tile(aw), tile(nh), tile(nh), tile(d), tile(d)]
        out_shape = [shape(cc, F32), shape(aw, BF16), shape(aw, F32), shape(aw, F32),
                     shape(nh, F32), shape(nh, F32), shape(d, F32), shape(d, F32)]
    in_specs = [tile(d), _const_spec((1, d))] + [_const_spec(m.shape) for m in weights] + [_const_spec((1, nh))]
    return pl.pallas_call(
        functools.partial(_inproj_kernel, seg=seg, q_scale=q_scale, prompt=prompt),
        grid=(nb, nt),
        in_specs=in_specs,
        out_specs=out_specs,
        out_shape=out_shape,
        scratch_shapes=[pltpu.VMEM((1, nh), F32)],
        compiler_params=_params("parallel", "arbitrary"),
        name="inproj",
    )(x3, g.reshape(1, d), *weights, b_forget.reshape(1, nh))


def _conv_kernel(x_ref, halo_ref, hist_ref, w_ref, b_ref, g_ref, beta_ref, wp_ref, o_ref,
                 buf_ref, sh_ref, y_ref, *, rows):
    i = pl.program_id(1)
    tt, ch = x_ref.shape[1], x_ref.shape[2]
    taps = w_ref.shape[0]
    buf_ref[0:CONV_HALO, :] = jnp.where(i == 0, hist_ref[0], halo_ref[0])
    buf_ref[CONV_HALO:CONV_HALO + tt, :] = x_ref[0]
    first = CONV_HALO - (taps - 1)

    for cb in range(ch // LANES):
        lanes = slice(cb * LANES, (cb + 1) * LANES)
        for r in range(SUBLANES):
            span = CONV_HALO + tt - (SUBLANES if r else 0)
            sh_ref[r, 0:span, :] = buf_ref[r:r + span, lanes]

        def body(rb, carry, lanes=lanes):
            r0 = pl.multiple_of(rb * rows, rows)
            acc = jnp.broadcast_to(b_ref[:, lanes], (rows, LANES))
            for j in range(taps):
                shift, base = (first + j) % SUBLANES, (first + j) // SUBLANES * SUBLANES
                acc = acc + sh_ref[shift, pl.ds(r0 + base, rows), :] * w_ref[j:j + 1, lanes]
            y_ref[pl.ds(r0, rows), lanes] = acc
            return carry

        lax.fori_loop(0, tt // rows, body, 0)

    y = y_ref[...]
    mu = jnp.mean(y, axis=-1, keepdims=True)
    dev = y - mu
    var = jnp.mean(dev * dev, axis=-1, keepdims=True)
    yn = dev * lax.rsqrt(var + EPS) * g_ref[...] + beta_ref[...]
    act = (yn * jax.nn.sigmoid(yn)).astype(BF16)
    o_ref[0] = _dot(act, wp_ref[...])


def _conv(glu, hist, w_dw, b_dw, ln_g, ln_b, wp, *, tt):
    nb, t, ch = glu.shape
    nt = t // tt
    taps = w_dw.shape[0]
    assert taps - 1 <= CONV_HALO and tt % SUBLANES == 0
    if nt > 1:
        assert tt % CONV_HALO == 0
        halo_src = glu
        halo_map = lambda b, i: (b, jnp.maximum(i * (tt // CONV_HALO) - 1, 0), 0)
    else:
        halo_src = hist
        halo_map = lambda b, i: (b, 0, 0)
    rows = min(tt, 32)
    return pl.pallas_call(
        functools.partial(_conv_kernel, rows=rows),
        grid=(nb, nt),
        in_specs=[pl.BlockSpec((1, tt, ch), lambda b, i: (b, i, 0)),
                  pl.BlockSpec((1, CONV_HALO, ch), halo_map),
                  pl.BlockSpec((1, CONV_HALO, ch), lambda b, i: (b, 0, 0)),
                  _const_spec((taps, ch)), _const_spec((1, ch)), _const_spec((1, ch)), _const_spec((1, ch)),
                  _const_spec(wp.shape)],
        out_specs=pl.BlockSpec((1, tt, wp.shape[1]), lambda b, i: (b, i, 0)),
        out_shape=jax.ShapeDtypeStruct((nb, t, wp.shape[1]), F32),
        scratch_shapes=[pltpu.VMEM((CONV_HALO + tt, ch), F32), pltpu.VMEM((SUBLANES, CONV_HALO + tt, LANES), F32),
                        pltpu.VMEM((tt, ch), F32)],
        compiler_params=_params("parallel", "parallel"),
        name="conv",
    )(glu, halo_src, hist, w_dw, b_dw.reshape(1, ch), ln_g.reshape(1, ch), ln_b.reshape(1, ch), wp)


def _attn_kernel(qt_ref, k_ref, kx_ref, vt_ref, cq_ref, o_ref, sa_ref, sb_ref, *, tq, head_dim):
    qi = pl.program_id(2)
    qt2 = qt_ref[0].astype(F32)
    sub = lax.broadcasted_iota(jnp.int32, (LANES, tq), 0)
    qaug = []
    for a in range(2):
        qm = jnp.where((sub >= head_dim) == bool(a), qt2, 0.0)
        ext = jnp.where((sub >= a * BIAS_PIECES) & (sub < (a + 1) * BIAS_PIECES), 1.0, 0.0)
        for p, piece in enumerate(_split3(cq_ref[0, 0, a:a + 1, :] * LOG2_E)):
            ext = jnp.where(sub == KEY_ONES_LANE + p, piece.astype(F32), ext)
        qaug.append(jnp.concatenate([qm, ext], axis=0).astype(BF16))
    tk = tq // 2
    ones_rows = jnp.ones((BF16_ROWS, tk), BF16)
    key = lax.broadcasted_iota(jnp.int32, (tk, tq), 0)
    qry = lax.broadcasted_iota(jnp.int32, (tk, tq), 1)

    def logits(j, s_ref):
        start = pl.multiple_of(j * tk, tk)
        kcat = jnp.concatenate([k_ref[0, pl.ds(start, tk), :], kx_ref[0, pl.ds(start, tk), :]], axis=1)
        for a in range(2):
            s_ref[a] = _dot(kcat, qaug[a])

    def absorb(j, s_ref, state, key_offset=None):
        start = pl.multiple_of(j * tk, tk)
        out = []
        for a in range(2):
            m, acc = state[a]
            sa = s_ref[a]
            if key_offset is not None:
                sa = jnp.where(key + key_offset <= qry, sa, -jnp.inf)
            m_new = jnp.maximum(m, jnp.max(sa, axis=0, keepdims=True))
            p = jnp.exp2(sa - m_new).astype(BF16)
            vta = jnp.concatenate([vt_ref[0, a * head_dim:(a + 1) * head_dim, pl.ds(start, tk)], ones_rows], axis=0)
            acc = jnp.exp2(m - m_new) * acc + _dot(vta, p)
            out.append((m_new, acc))
        return tuple(out)

    def body(jj, state):
        logits(2 * jj + 1, sb_ref)
        state = absorb(2 * jj, sa_ref, state)
        logits(2 * jj + 2, sa_ref)
        return absorb(2 * jj + 1, sb_ref, state)

    init = tuple((jnp.full((1, tq), -jnp.inf, F32), jnp.zeros((head_dim + BF16_ROWS, tq), F32)) for _ in range(2))
    logits(0, sa_ref)
    state = lax.fori_loop(0, qi, body, init)
    logits(2 * qi + 1, sb_ref)
    state = absorb(2 * qi, sa_ref, state, key_offset=0)
    (_, acc0), (_, acc1) = absorb(2 * qi + 1, sb_ref, state, key_offset=tk)
    heads = [acc[:head_dim] / acc[head_dim:head_dim + 1] for acc in (acc0, acc1)]
    o_ref[0] = jnp.concatenate(heads, axis=0).T.astype(o_ref.dtype)


def _attn(qt, k, kx, vt, c, *, tq, head_dim):
    nb, width, t = qt.shape
    nh = width // head_dim
    per = LANES // head_dim
    assert per == 2 and nh % per == 0 and t % tq == 0
    npair = nh // per
    cq = c.reshape(nb, t, npair, per).transpose(0, 2, 3, 1)
    return pl.pallas_call(
        functools.partial(_attn_kernel, tq=tq, head_dim=head_dim),
        grid=(nb, npair, t // tq),
        in_specs=[pl.BlockSpec((1, LANES, tq), lambda b, hp, i: (b, hp, i)),
                  pl.BlockSpec((1, t, LANES), lambda b, hp, i: (b, 0, hp)),
                  pl.BlockSpec((1, t, LANES), lambda b, hp, i: (b, 0, hp)),
                  pl.BlockSpec((1, LANES, t), lambda b, hp, i: (b, hp, 0)),
                  pl.BlockSpec((1, 1, per, tq), lambda b, hp, i: (b, hp, 0, i))],
        out_specs=pl.BlockSpec((1, tq, LANES), lambda b, hp, i: (b, i, hp)),
        out_shape=jax.ShapeDtypeStruct((nb, t, width), BF16),
        scratch_shapes=[pltpu.VMEM((per, tq // 2, tq), F32), pltpu.VMEM((per, tq // 2, tq), F32)],
        compiler_params=_params("parallel", "parallel", "arbitrary"),
        name="attn",
    )(qt, k, kx, vt, cq)


def _decode_kernel(pt_ref, qa_ref, knt_ref, vnt_ref, *rest, pages_per_step):
    pp = pages_per_step
    k_refs, v_refs, lf_refs = rest[:pp], rest[pp:2 * pp], rest[2 * pp:3 * pp]
    o_ref = rest[3 * pp]
    m_ref, l_ref, acc_ref, carry_ref = rest[3 * pp + 1:]
    i = pl.program_id(1)
    page = k_refs[0].shape[3]
    nrow = qa_ref.shape[1]
    nh = lf_refs[0].shape[2]
    nq = nrow // nh
    pair_rows = 2 * nq
    qa = qa_ref[0]

    def update(logits_list, vt_list):
        m_old = m_ref[...]
        m_new = m_old
        for lg in logits_list:
            m_new = jnp.maximum(m_new, jnp.max(lg, axis=1, keepdims=True))
        alpha = jnp.exp(m_old - m_new)
        ps = [jnp.exp(lg - m_new) for lg in logits_list]
        l_new = alpha * l_ref[...]
        for p in ps:
            l_new = l_new + jnp.sum(p, axis=1, keepdims=True)
        l_ref[...] = l_new
        m_ref[...] = m_new
        ps = [p.astype(BF16) for p in ps]
        pcat = ps[0] if len(ps) == 1 else jnp.concatenate(ps, axis=1)
        vcat = vt_list[0] if len(vt_list) == 1 else jnp.concatenate(vt_list, axis=1)
        for j in range(nrow // pair_rows):
            rws = slice(j * pair_rows, (j + 1) * pair_rows)
            pv = _dot_nt(pcat[rws, :], vcat[j * LANES:(j + 1) * LANES, :])
            acc_ref[rws, :] = alpha[rws, :] * acc_ref[rws, :] + pv

    @pl.when(i == 0)
    def _():
        m_ref[...] = jnp.full_like(m_ref, -jnp.inf)
        l_ref[...] = jnp.zeros_like(l_ref)
        acc_ref[...] = jnp.zeros_like(acc_ref)
        carry_ref[...] = jnp.zeros_like(carry_ref)
        kpos = lax.broadcasted_iota(jnp.int32, (nrow, page), 1)
        qpos = jnp.bitwise_and(lax.broadcasted_iota(jnp.int32, (nrow, page), 0), nq - 1)
        update([jnp.where(kpos <= qpos, _dot(qa, knt_ref[0]), -jnp.inf)], [vnt_ref[0]])

    later = jnp.where(lax.broadcasted_iota(jnp.int32, (page, page), 0)
                      > lax.broadcasted_iota(jnp.int32, (page, page), 1), 1.0, 0.0).astype(BF16)
    ones_rows = jnp.where(lax.broadcasted_iota(jnp.int32, (BF16_ROWS, page), 0) < BIAS_PIECES, 1.0, 0.0).astype(BF16)
    carry = carry_ref[...]
    logits_list, vt_list = [], []
    for j in range(pp):
        lft = lf_refs[j][0, 0]
        r3 = _dot(jnp.concatenate(_split3(lft), axis=0), later)
        suffix = r3[0:nh] + r3[nh:2 * nh] + r3[2 * nh:3 * nh] + carry
        carry = carry + jnp.sum(lft, axis=1, keepdims=True)
        kt_aug = jnp.concatenate([k_refs[j][0, 0].astype(BF16), *_split3(suffix), ones_rows], axis=0)
        logits_list.append(_dot(qa, kt_aug))
        vt_list.append(v_refs[j][0, 0].astype(BF16))
    carry_ref[...] = carry
    update(logits_list, vt_list)

    @pl.when(i == pl.num_programs(1) - 1)
    def _():
        head_of_row = jnp.right_shift(lax.broadcasted_iota(jnp.int32, (nrow, LANES), 0), int(math.log2(nq)))
        half_of_lane = jnp.right_shift(lax.broadcasted_iota(jnp.int32, (nrow, LANES), 1), int(math.log2(LANES // 2)))
        own_half = jnp.bitwise_and(head_of_row, 1) == half_of_lane
        o_ref[0] = jnp.where(own_half, acc_ref[...] / l_ref[...], 0.0)


def _pieces(x, axis):
    return jnp.stack(_split3(x), axis=axis)


def _decode_attn(q, k_new, v_new, c_new, cache_k, cache_v, cache_logf, page_table, layer, *, pages_per_step):
    db, s, width = q.shape
    _, n_pool, page, nh, head_dim = cache_k.shape
    n_pages = page_table.shape[1]
    nrow = nh * s
    pp = pages_per_step
    assert nrow == LANES and page == LANES and n_pages % pp == 0 and LANES // head_dim == 2
    assert s & (s - 1) == 0 and s <= page and nh == BF16_ROWS
    kt_cache = cache_k.transpose(0, 1, 3, 4, 2).reshape(-1, n_pool, width, page)
    vt_cache = cache_v.transpose(0, 1, 3, 4, 2).reshape(-1, n_pool, width, page)
    lft_cache = cache_logf.transpose(0, 1, 3, 2)
    eye = jnp.eye(nh, dtype=BF16)
    qh = q.reshape(db, s, nh, head_dim).transpose(0, 2, 1, 3)
    qbd = (qh[:, :, :, None, :] * eye[None, :, None, :, None]).reshape(db, nrow, width)
    pick = jnp.broadcast_to(jnp.tile(jnp.repeat(eye, s, axis=0), (1, BIAS_PIECES))[None], (db, nrow, BIAS_PIECES * nh))
    c_rows = c_new.transpose(0, 2, 1).reshape(db, nrow)
    c_cols = jnp.pad(_pieces(c_rows, 2), ((0, 0), (0, 0), (0, BF16_ROWS - BIAS_PIECES)))
    qa = jnp.concatenate([qbd, pick, c_cols], axis=2)
    lane_pad = ((0, 0), (0, 0), (0, page - s))
    neg_c = jnp.pad(-c_new.transpose(0, 2, 1), lane_pad)
    ones = jnp.broadcast_to((jnp.arange(BF16_ROWS) < BIAS_PIECES).astype(BF16)[None, :, None], (db, BF16_ROWS, page))
    knt = jnp.concatenate([jnp.pad(k_new.transpose(0, 2, 1), lane_pad).astype(BF16),
                           _pieces(neg_c, 1).reshape(db, BIAS_PIECES * nh, page), ones], axis=1)
    vnt = jnp.pad(v_new.transpose(0, 2, 1), lane_pad).astype(BF16)
    ka = qa.shape[2]

    def page_map(j):
        return lambda b, i, pt: (layer, pt[b * n_pages + n_pages - 1 - (i * pp + j)], 0, 0)

    per_seq = lambda shape: pl.BlockSpec((1,) + shape, lambda b, i, pt: (b, 0, 0))
    in_specs = ([per_seq((nrow, ka)), per_seq((ka, page)), per_seq((width, page))]
                + [pl.BlockSpec((1, 1, width, page), page_map(j)) for j in range(pp)]
                + [pl.BlockSpec((1, 1, width, page), page_map(j)) for j in range(pp)]
                + [pl.BlockSpec((1, 1, nh, page), page_map(j)) for j in range(pp)])
    out = pl.pallas_call(
        functools.partial(_decode_kernel, pages_per_step=pp),
        grid_spec=pltpu.PrefetchScalarGridSpec(
            num_scalar_prefetch=1,
            grid=(db, n_pages // pp),
            in_specs=in_specs,
            out_specs=pl.BlockSpec((1, nrow, LANES), lambda b, i, pt: (b, 0, 0)),
            scratch_shapes=[pltpu.VMEM((nrow, 1), F32), pltpu.VMEM((nrow, 1), F32),
                            pltpu.VMEM((nrow, LANES), F32), pltpu.VMEM((nh, 1), F32)]),
        out_shape=jax.ShapeDtypeStruct((db, nrow, LANES), F32),
        compiler_params=_params("parallel", "arbitrary"),
        name="decode",
    )(page_table.reshape(-1), qa, knt, vnt, *([kt_cache] * pp), *([vt_cache] * pp), *([lft_cache] * pp))
    o = out.reshape(db, nh // 2, 2, s, 2, head_dim)
    o = jnp.stack([o[:, :, 0, :, 0, :], o[:, :, 1, :, 1, :]], axis=2)
    return o.transpose(0, 3, 1, 2, 4).reshape(db, s, width).astype(BF16)


def _merge_kernel(x_ref, conv_ref, attn_ref, gc_ref, ga_ref, wpa_ref, wo_ref, o_ref):
    attn_y = _dot(attn_ref[...], wpa_ref[...])
    m = gc_ref[...] * conv_ref[...] + ga_ref[...] * attn_y
    o_ref[...] = x_ref[...] + _dot(m.astype(BF16), wo_ref[...])


def _merge(x2d, conv_y, attn, gc, ga, wpa, wo, *, tm):
    n, d = x2d.shape
    aw = attn.shape[1]
    tile = lambda width: pl.BlockSpec((tm, width), lambda i: (i, 0))
    return pl.pallas_call(
        _merge_kernel,
        grid=(n // tm,),
        in_specs=[tile(d), tile(d), tile(aw), tile(d), tile(d), _const_spec(wpa.shape), _const_spec(wo.shape)],
        out_specs=tile(d),
        out_shape=jax.ShapeDtypeStruct((n, d), F32),
        compiler_params=_params("parallel"),
        name="merge",
    )(x2d, conv_y, attn, gc, ga, wpa, wo)


def _token_tile(n, want):
    tm = min(n, want)
    assert n % tm == 0
    return tm


def kernel(x_prompt, x_sample, cache_k, cache_v, cache_logf, state_conv, page_table, norm_ffn1, w_ffn1_gate, w_ffn1_up, w_ffn1_down, norm_mix, w_in, b_forget, w_dw, b_dw, ln_conv_g, ln_conv_b, w_proj_conv, w_proj_attn, w_out, norm_ffn2, w_ffn2_gate, w_ffn2_up, w_ffn2_down, norm_final):
    depth = norm_ffn1.shape[0]
    nb, t, d = x_prompt.shape
    db, s, _ = x_sample.shape
    nh, head_dim = cache_k.shape[3], cache_k.shape[4]
    aw = nh * head_dim
    cc = w_dw.shape[2]
    taps = w_dw.shape[1]
    n_pool, page = cache_k.shape[1], cache_k.shape[2]
    q_scale = 1.0 / math.sqrt(head_dim)
    assert q_scale == 2.0 ** round(math.log2(q_scale))

    xp, xs = x_prompt, x_sample.reshape(1, db * s, d)
    outs = [[] for _ in range(8)]
    for l in range(depth):
        bf = lambda a: a[l].astype(BF16)
        w1 = (bf(w_ffn1_gate), bf(w_ffn1_up), bf(w_ffn1_down))
        w2 = (bf(w_ffn2_gate), bf(w_ffn2_up), bf(w_ffn2_down))
        wi = w_in[l].astype(BF16)
        edges = [0, cc, 2 * cc, 2 * cc + aw, 2 * cc + 2 * aw, 2 * cc + 3 * aw, 2 * cc + 3 * aw + nh,
                 2 * cc + 3 * aw + nh + d, 2 * cc + 3 * aw + nh + 2 * d]
        w_split = tuple(wi[:, a:b] for a, b in zip(edges[:-1], edges[1:]))
        wpc, wpa, wo = bf(w_proj_conv), bf(w_proj_attn), bf(w_out)
        last = l == depth - 1

        def half_layer(x3, seg, tm_ffn, tm_in, prompt):
            b3, t3, _ = x3.shape
            x1 = _ffn(x3.reshape(b3 * t3, d), norm_ffn1[l], *w1, tm=tm_ffn).reshape(b3, t3, d)
            return x1, _inproj(x1, norm_mix[l], w_split, b_forget[l], tm=tm_in, seg=seg, q_scale=q_scale,
                               prompt=prompt)

        def finish(x1, conv_y, attn, gc, ga, tm):
            n = x1.shape[0] * x1.shape[1]
            x2 = _merge(x1.reshape(n, d), conv_y.reshape(n, d), attn.reshape(n, aw), gc.reshape(n, d),
                        ga.reshape(n, d), wpa, wo, tm=tm)
            return _ffn(x2, norm_ffn2[l], *w2, norm_final if last else None, tm=tm)

        tm_p = _token_tile(t, 256)
        x1, (glu, qt, kt, vt32, kb, kx, vt, lf, c, gc, ga) = half_layer(xp, t, _token_tile(nb * t, 512), tm_p, True)
        hist0 = jnp.zeros((nb, CONV_HALO, cc), F32)
        conv_y = _conv(glu, hist0, w_dw[l], b_dw[l], ln_conv_g[l], ln_conv_b[l], wpc, tt=_token_tile(t, 512))
        attn = _attn(qt, kb, kx, vt, c, tq=_token_tile(t, 512), head_dim=head_dim)
        xp = finish(x1, conv_y, attn, gc, ga, _token_tile(nb * t, 256)).reshape(nb, t, d)
        if last:
            y_prompt = xp
        outs[0].append(kt.reshape(nb, nh, head_dim, t).transpose(0, 3, 1, 2))
        outs[1].append(vt32.reshape(nb, nh, head_dim, t).transpose(0, 3, 1, 2))
        outs[2].append(lf)
        outs[3].append(glu[:, t - (taps - 1):])

        n_s = db * s
        x1, (glu, q, k, v, lf, c, gc, ga) = half_layer(xs, s, n_s, n_s, False)
        glu = glu.reshape(db, s, cc)
        state = state_conv[l].astype(F32)
        hist = jnp.pad(state, ((0, 0), (CONV_HALO - (taps - 1), 0), (0, 0)))
        conv_y = _conv(glu, hist, w_dw[l], b_dw[l], ln_conv_g[l], ln_conv_b[l], wpc, tt=s)
        attn = _decode_attn(q.reshape(db, s, aw), k.reshape(db, s, aw), v.reshape(db, s, aw),
                            c.reshape(db, s, nh), cache_k, cache_v, cache_logf, page_table, l,
                            pages_per_step=8)
        xs = finish(x1, conv_y, attn, gc, ga, n_s).reshape(1, n_s, d)
        if last:
            y_sample = xs.reshape(db, s, d)
        outs[4].append(k.reshape(db, s, nh, head_dim))
        outs[5].append(v.reshape(db, s, nh, head_dim))
        outs[6].append(lf.reshape(db, s, nh))
        outs[7].append(jnp.concatenate([state, glu], axis=1)[:, s:])

    stacked = [jnp.stack(o) for o in outs]
    return (y_prompt, y_sample, *stacked)
```

```python
import functools
import math

import jax
import jax.numpy as jnp
from jax import lax
from jax.experimental import pallas as pl
from jax.experimental.pallas import tpu as pltpu

F32 = jnp.float32
BF16 = jnp.bfloat16
EPS = 1e-6
LANES = 128
SUBLANES = 8
BF16_ROWS = 16
VMEM_LIMIT_BYTES = 56 * 2 ** 20
CONV_HALO = 32


def _dot(a, b):
    return jnp.dot(a, b, preferred_element_type=F32)


def _rms(x, g):
    return x * lax.rsqrt(jnp.mean(x * x, axis=-1, keepdims=True) + EPS) * g


def _split3(x):
    hi = x.astype(BF16)
    r = x - hi.astype(F32)
    mid = r.astype(BF16)
    lo = (r - mid.astype(F32)).astype(BF16)
    return hi, mid, lo


def _dot_exact_lhs01(m01, x):
    hi, mid, lo = _split3(x)
    return _dot(m01, hi) + _dot(m01, mid) + _dot(m01, lo)


def _dot_exact_rhs01(x, m01):
    hi, mid, lo = _split3(x)
    return _dot(hi, m01) + _dot(mid, m01) + _dot(lo, m01)


def _const_spec(shape):
    nd = len(shape)
    return pl.BlockSpec(shape, lambda *_: (0,) * nd, pipeline_mode=pl.Buffered(1))


def _params(*semantics):
    return pltpu.CompilerParams(dimension_semantics=semantics, vmem_limit_bytes=VMEM_LIMIT_BYTES)


def _ffn_kernel(*refs, final_norm, merge):
    refs = list(refs)
    x_ref = refs.pop(0)
    if merge:
        conv_ref, attn_ref, gc_ref, ga_ref, wpa_ref, wo_ref = refs[:6]
        del refs[:6]
    if final_norm:
        g_ref, wg_ref, wu_ref, wd_ref, gf_ref, o_ref = refs
    else:
        g_ref, wg_ref, wu_ref, wd_ref, o_ref = refs
    x = x_ref[...]
    if merge:
        mixed = gc_ref[...] * conv_ref[...] + ga_ref[...] * _dot(attn_ref[...], wpa_ref[...])
        x = x + _dot(mixed.astype(BF16), wo_ref[...])
    h = _rms(x, g_ref[...]).astype(BF16)
    gate = _dot(h, wg_ref[...])
    up = _dot(h, wu_ref[...])
    act = (gate * jax.nn.sigmoid(gate) * up).astype(BF16)
    y = x + 0.5 * _dot(act, wd_ref[...])
    if final_norm:
        y = _rms(y, gf_ref[...])
    o_ref[...] = y


def _ffn(x2d, g, wg, wu, wd, gf=None, *, tm, merge=None):
    n, d = x2d.shape
    f = wg.shape[1]
    tile = lambda width: pl.BlockSpec((tm, width), lambda i: (i, 0))
    in_specs, args = [tile(d)], [x2d]
    if merge is not None:
        conv_y, attn, gc, ga, wpa, wo = merge
        in_specs += [tile(d), tile(attn.shape[1]), tile(d), tile(d), _const_spec(wpa.shape), _const_spec(wo.shape)]
        args += [conv_y, attn, gc, ga, wpa, wo]
    in_specs += [_const_spec((1, d)), _const_spec((d, f)), _const_spec((d, f)), _const_spec((f, d))]
    args += [g.reshape(1, d), wg, wu, wd]
    if gf is not None:
        in_specs.append(_const_spec((1, d)))
        args.append(gf.reshape(1, d))
    return pl.pallas_call(
        functools.partial(_ffn_kernel, final_norm=gf is not None, merge=merge is not None),
        grid=(n // tm,),
        in_specs=in_specs,
        out_specs=pl.BlockSpec((tm, d), lambda i: (i, 0)),
        out_shape=jax.ShapeDtypeStruct((n, d), F32),
        compiler_params=_params("parallel"),
        name="ffn",
    )(*args)


LOG2_E = math.log2(math.e)
BIAS_PIECES = 3
KEY_ONES_LANE = 2 * BIAS_PIECES


def _dot_nt(a, b):
    return lax.dot_general(a, b, (((1,), (1,)), ((), ())), preferred_element_type=F32)


def _inproj_kernel(*refs, seg, q_scale, prompt):
    if prompt:
        (x_ref, g_ref, wa_ref, wb_ref, wqt_ref, wkt_ref, wvt_ref, wf_ref, wgc_ref, wga_ref, bf_ref,
         glu_ref, qt_ref, kt_ref, vt_ref, kb_ref, kx_ref, vtb_ref, lf_ref, c_ref, gc_ref, ga_ref, carry_ref) = refs
    else:
        (x_ref, g_ref, wa_ref, wb_ref, wq_ref, wk_ref, wv_ref, wf_ref, wgc_ref, wga_ref, bf_ref,
         glu_ref, q_ref, k_ref, v_ref, lf_ref, c_ref, gc_ref, ga_ref, carry_ref) = refs
    t = pl.program_id(1)
    tm = x_ref.shape[1]
    h = _rms(x_ref[0], g_ref[...]).astype(BF16)
    glu_ref[0] = _dot(h, wa_ref[...]) * jax.nn.sigmoid(_dot(h, wb_ref[...]))
    if prompt:
        qt_ref[0] = (_dot_nt(wqt_ref[...], h) * (q_scale * LOG2_E)).astype(BF16)
        kt = _dot_nt(wkt_ref[...], h)
        kt_ref[0] = kt
        kb_ref[0] = kt.T.astype(BF16)
        vt = _dot_nt(wvt_ref[...], h)
        vt_ref[0] = vt
        vtb_ref[0] = vt.astype(BF16)
    else:
        q_ref[0] = (_dot(h, wq_ref[...]) * q_scale).astype(BF16)
        k_ref[0] = _dot(h, wk_ref[...])
        v_ref[0] = _dot(h, wv_ref[...])
    gc_ref[0] = jax.nn.sigmoid(_dot(h, wgc_ref[...]))
    ga_ref[0] = jax.nn.sigmoid(_dot(h, wga_ref[...]))

    lf = jax.nn.log_sigmoid(_dot(h, wf_ref[...]) + bf_ref[...])
    lf_ref[0] = lf
    row = lax.broadcasted_iota(jnp.int32, (tm, tm), 0)
    col = lax.broadcasted_iota(jnp.int32, (tm, tm), 1)
    keep = col <= row
    if seg < tm:
        shift = int(math.log2(seg))
        keep = keep & (jnp.right_shift(row, shift) == jnp.right_shift(col, shift))
    tri = jnp.where(keep, 1.0, 0.0).astype(BF16)

    @pl.when(t == 0)
    def _():
        carry_ref[...] = jnp.zeros_like(carry_ref)

    c = _dot_exact_lhs01(tri, lf) + carry_ref[...]
    c_ref[0] = c
    carry_ref[...] = c_ref[0, tm - 1:tm, :]

    if prompt:
        nh, width = lf.shape[1], kx_ref.shape[2]
        head = lax.broadcasted_iota(jnp.int32, (nh, width), 0)
        lane = lax.broadcasted_iota(jnp.int32, (nh, width), 1)
        in_pair = jnp.right_shift(lane, int(math.log2(LANES))) == jnp.right_shift(head, 1)
        slot = jnp.bitwise_and(lane, LANES - 1) - jnp.bitwise_and(head, 1) * BIAS_PIECES
        ext = None
        for p, piece in enumerate(_split3(c * LOG2_E)):
            place = jnp.where(in_pair & (slot == p), -1.0, 0.0).astype(BF16)
            term = _dot(piece, place)
            ext = term if ext is None else ext + term
        within = jnp.bitwise_and(lax.broadcasted_iota(jnp.int32, (1, width), 1), LANES - 1)
        ones = jnp.where((within >= KEY_ONES_LANE) & (within < KEY_ONES_LANE + BIAS_PIECES), 1.0, 0.0)
        kx_ref[0] = (ext + ones).astype(BF16)


def _inproj(x3, g, w, b_forget, *, tm, seg, q_scale, prompt):
    nb, t, d = x3.shape
    nt = t // tm
    assert seg >= tm or (nt == 1 and tm % seg == 0 and seg & (seg - 1) == 0)
    wa, wb, wq, wk, wv, wf, wgc, wga = w
    cc, aw, nh = wa.shape[1], wq.shape[1], wf.shape[1]
    tile = lambda width: pl.BlockSpec((1, tm, width), lambda b, i: (b, i, 0))
    tile_t = lambda width: pl.BlockSpec((1, width, tm), lambda b, i: (b, 0, i))
    shape = lambda width, dt: jax.ShapeDtypeStruct((nb, t, width), dt)
    shape_t = lambda width, dt: jax.ShapeDtypeStruct((nb, width, t), dt)
    if prompt:
        assert 2 * (aw // nh) == LANES and KEY_ONES_LANE + BIAS_PIECES <= LANES
        weights = (wa, wb, wq.T, wk.T, wv.T, wf, wgc, wga)
        out_specs = [tile(cc), tile_t(aw), tile_t(aw), tile_t(aw), tile(aw), tile(aw), tile_t(aw),
                     tile(nh), tile(nh), tile(d), tile(d)]
        out_shape = [shape(cc, F32), shape_t(aw, BF16), shape_t(aw, F32), shape_t(aw, F32), shape(aw, BF16),
                     shape(aw, BF16), shape_t(aw, BF16), shape(nh, F32), shape(nh, F32), shape(d, F32), shape(d, F32)]
    else:
        weights = w
        out_specs = [tile(cc), tile(aw), tile(aw), tile(aw), tile(nh), tile(nh), tile(d), tile(d)]
        out_shape = [shape(cc, F32), shape(aw, BF16), shape(aw, F32), shape(aw, F32),
                     shape(nh, F32), shape(nh, F32), shape(d, F32), shape(d, F32)]
    in_specs = [tile(d), _const_spec((1, d))] + [_const_spec(m.shape) for m in weights] + [_const_spec((1, nh))]
    return pl.pallas_call(
        functools.partial(_inproj_kernel, seg=seg, q_scale=q_scale, prompt=prompt),
        grid=(nb, nt),
        in_specs=in_specs,
        out_specs=out_specs,
        out_shape=out_shape,
        scratch_shapes=[pltpu.VMEM((1, nh), F32)],
        compiler_params=_params("parallel", "arbitrary"),
        name="inproj",
    )(x3, g.reshape(1, d), *weights, b_forget.reshape(1, nh))


def _conv_kernel(x_ref, halo_ref, hist_ref, w_ref, b_ref, g_ref, beta_ref, wp_ref, o_ref,
                 buf_ref, sh_ref, y_ref, *, rows):
    i = pl.program_id(1)
    tt, ch = x_ref.shape[1], x_ref.shape[2]
    taps = w_ref.shape[0]
    buf_ref[0:CONV_HALO, :] = jnp.where(i == 0, hist_ref[0], halo_ref[0])
    buf_ref[CONV_HALO:CONV_HALO + tt, :] = x_ref[0]
    first = CONV_HALO - (taps - 1)

    for cb in range(ch // LANES):
        lanes = slice(cb * LANES, (cb + 1) * LANES)
        for r in range(SUBLANES):
            span = CONV_HALO + tt - (SUBLANES if r else 0)
            sh_ref[r, 0:span, :] = buf_ref[r:r + span, lanes]

        def body(rb, carry, lanes=lanes):
            r0 = pl.multiple_of(rb * rows, rows)
            acc = jnp.broadcast_to(b_ref[:, lanes], (rows, LANES))
            for j in range(taps):
                shift, base = (first + j) % SUBLANES, (first + j) // SUBLANES * SUBLANES
                acc = acc + sh_ref[shift, pl.ds(r0 + base, rows), :] * w_ref[j:j + 1, lanes]
            y_ref[pl.ds(r0, rows), lanes] = acc
            return carry

        lax.fori_loop(0, tt // rows, body, 0)

    y = y_ref[...]
    mu = jnp.mean(y, axis=-1, keepdims=True)
    dev = y - mu
    var = jnp.mean(dev * dev, axis=-1, keepdims=True)
    yn = dev * lax.rsqrt(var + EPS) * g_ref[...] + beta_ref[...]
    act = (yn * jax.nn.sigmoid(yn)).astype(BF16)
    o_ref[0] = _dot(act, wp_ref[...])


def _conv(glu, hist, w_dw, b_dw, ln_g, ln_b, wp, *, tt):
    nb, t, ch = glu.shape
    nt = t // tt
    taps = w_dw.shape[0]
    assert taps - 1 <= CONV_HALO and tt % SUBLANES == 0
    if nt > 1:
        assert tt % CONV_HALO == 0
        halo_src = glu
        halo_map = lambda b, i: (b, jnp.maximum(i * (tt // CONV_HALO) - 1, 0), 0)
    else:
        halo_src = hist
        halo_map = lambda b, i: (b, 0, 0)
    rows = min(tt, 32)
    return pl.pallas_call(
        functools.partial(_conv_kernel, rows=rows),
        grid=(nb, nt),
        in_specs=[pl.BlockSpec((1, tt, ch), lambda b, i: (b, i, 0)),
                  pl.BlockSpec((1, CONV_HALO, ch), halo_map),
                  pl.BlockSpec((1, CONV_HALO, ch), lambda b, i: (b, 0, 0)),
                  _const_spec((taps, ch)), _const_spec((1, ch)), _const_spec((1, ch)), _const_spec((1, ch)),
                  _const_spec(wp.shape)],
        out_specs=pl.BlockSpec((1, tt, wp.shape[1]), lambda b, i: (b, i, 0)),
        out_shape=jax.ShapeDtypeStruct((nb, t, wp.shape[1]), F32),
        scratch_shapes=[pltpu.VMEM((CONV_HALO + tt, ch), F32), pltpu.VMEM((SUBLANES, CONV_HALO + tt, LANES), F32),
                        pltpu.VMEM((tt, ch), F32)],
        compiler_params=_params("parallel", "parallel"),
        name="conv",
    )(glu, halo_src, hist, w_dw, b_dw.reshape(1, ch), ln_g.reshape(1, ch), ln_b.reshape(1, ch), wp)


def _attn_kernel(qt_ref, k_ref, kx_ref, vt_ref, cq_ref, o_ref, sa_ref, sb_ref, *, tq, tk, head_dim):
    qi = pl.program_id(2)
    qt2 = qt_ref[0].astype(F32)
    sub = lax.broadcasted_iota(jnp.int32, (LANES, tq), 0)
    qaug = []
    for a in range(2):
        qm = jnp.where((sub >= head_dim) == bool(a), qt2, 0.0)
        ext = jnp.where((sub >= a * BIAS_PIECES) & (sub < (a + 1) * BIAS_PIECES), 1.0, 0.0)
        for p, piece in enumerate(_split3(cq_ref[0, 0, a:a + 1, :] * LOG2_E)):
            ext = jnp.where(sub == KEY_ONES_LANE + p, piece.astype(F32), ext)
        qaug.append(jnp.concatenate([qm, ext], axis=0).astype(BF16))
    n_sub = tq // tk
    ones_rows = jnp.ones((BF16_ROWS, tk), BF16)
    key = lax.broadcasted_iota(jnp.int32, (tk, tq), 0)
    qry = lax.broadcasted_iota(jnp.int32, (tk, tq), 1)

    def logits(j, s_ref):
        start = pl.multiple_of(j * tk, tk)
        kcat = jnp.concatenate([k_ref[0, pl.ds(start, tk), :], kx_ref[0, pl.ds(start, tk), :]], axis=1)
        for a in range(2):
            s_ref[a] = _dot(kcat, qaug[a])

    def absorb(j, s_ref, state, key_offset=None):
        start = pl.multiple_of(j * tk, tk)
        out = []
        for a in range(2):
            m, acc = state[a]
            sa = s_ref[a]
            if key_offset is not None:
                sa = jnp.where(key + key_offset <= qry, sa, -jnp.inf)
            m_new = jnp.maximum(m, jnp.max(sa, axis=0, keepdims=True))
            p = jnp.exp2(sa - m_new).astype(BF16)
            vta = jnp.concatenate([vt_ref[0, a * head_dim:(a + 1) * head_dim, pl.ds(start, tk)], ones_rows], axis=0)
            acc = jnp.exp2(m - m_new) * acc + _dot(vta, p)
            out.append((m_new, acc))
        return tuple(out)

    bufs = (sa_ref, sb_ref)

    def tile_of_chunks(first, state, diagonal):
        for c in range(n_sub):
            if not (diagonal and c == n_sub - 1):
                logits(first + c + 1, bufs[(c + 1) % 2])
            state = absorb(first + c, bufs[c % 2], state, key_offset=c * tk if diagonal else None)
        return state

    init = tuple((jnp.full((1, tq), -jnp.inf, F32), jnp.zeros((head_dim + BF16_ROWS, tq), F32)) for _ in range(2))
    logits(0, sa_ref)
    state = lax.fori_loop(0, qi, lambda jj, st: tile_of_chunks(jj * n_sub, st, False), init)
    (_, acc0), (_, acc1) = tile_of_chunks(qi * n_sub, state, True)
    heads = [acc[:head_dim] / acc[head_dim:head_dim + 1] for acc in (acc0, acc1)]
    o_ref[0] = jnp.concatenate(heads, axis=0).T.astype(o_ref.dtype)


def _attn(qt, k, kx, vt, c, *, tq, tk, head_dim):
    nb, width, t = qt.shape
    nh = width // head_dim
    per = LANES // head_dim
    assert per == 2 and nh % per == 0 and t % tq == 0 and tq % (2 * tk) == 0
    npair = nh // per
    cq = c.reshape(nb, t, npair, per).transpose(0, 2, 3, 1)
    return pl.pallas_call(
        functools.partial(_attn_kernel, tq=tq, tk=tk, head_dim=head_dim),
        grid=(nb, npair, t // tq),
        in_specs=[pl.BlockSpec((1, LANES, tq), lambda b, hp, i: (b, hp, i)),
                  pl.BlockSpec((1, t, LANES), lambda b, hp, i: (b, 0, hp)),
                  pl.BlockSpec((1, t, LANES), lambda b, hp, i: (b, 0, hp)),
                  pl.BlockSpec((1, LANES, t), lambda b, hp, i: (b, hp, 0)),
                  pl.BlockSpec((1, 1, per, tq), lambda b, hp, i: (b, hp, 0, i))],
        out_specs=pl.BlockSpec((1, tq, LANES), lambda b, hp, i: (b, i, hp)),
        out_shape=jax.ShapeDtypeStruct((nb, t, width), BF16),
        scratch_shapes=[pltpu.VMEM((per, tk, tq), F32), pltpu.VMEM((per, tk, tq), F32)],
        compiler_params=_params("parallel", "parallel", "arbitrary"),
        name="attn",
    )(qt, k, kx, vt, cq)


def _decode_kernel(pt_ref, qa_ref, knt_ref, vnt_ref, *rest, pages_per_step):
    pp = pages_per_step
    k_refs, v_refs, lf_refs = rest[:pp], rest[pp:2 * pp], rest[2 * pp:3 * pp]
    o_ref = rest[3 * pp]
    m_ref, l_ref, acc_ref, carry_ref = rest[3 * pp + 1:]
    i = pl.program_id(1)
    page = k_refs[0].shape[3]
    nrow = qa_ref.shape[1]
    nh = lf_refs[0].shape[2]
    nq = nrow // nh
    pair_rows = 2 * nq
    qa = qa_ref[0]

    def update(logits_list, vt_list):
        m_old = m_ref[...]
        m_new = m_old
        for lg in logits_list:
            m_new = jnp.maximum(m_new, jnp.max(lg, axis=1, keepdims=True))
        alpha = jnp.exp(m_old - m_new)
        ps = [jnp.exp(lg - m_new) for lg in logits_list]
        l_new = alpha * l_ref[...]
        for p in ps:
            l_new = l_new + jnp.sum(p, axis=1, keepdims=True)
        l_ref[...] = l_new
        m_ref[...] = m_new
        ps = [p.astype(BF16) for p in ps]
        pcat = ps[0] if len(ps) == 1 else jnp.concatenate(ps, axis=1)
        vcat = vt_list[0] if len(vt_list) == 1 else jnp.concatenate(vt_list, axis=1)
        for j in range(nrow // pair_rows):
            rws = slice(j * pair_rows, (j + 1) * pair_rows)
            pv = _dot_nt(pcat[rws, :], vcat[j * LANES:(j + 1) * LANES, :])
            acc_ref[rws, :] = alpha[rws, :] * acc_ref[rws, :] + pv

    @pl.when(i == 0)
    def _():
        m_ref[...] = jnp.full_like(m_ref, -jnp.inf)
        l_ref[...] = jnp.zeros_like(l_ref)
        acc_ref[...] = jnp.zeros_like(acc_ref)
        carry_ref[...] = jnp.zeros_like(carry_ref)
        kpos = lax.broadcasted_iota(jnp.int32, (nrow, page), 1)
        qpos = jnp.bitwise_and(lax.broadcasted_iota(jnp.int32, (nrow, page), 0), nq - 1)
        update([jnp.where(kpos <= qpos, _dot(qa, knt_ref[0]), -jnp.inf)], [vnt_ref[0]])

    later = jnp.where(lax.broadcasted_iota(jnp.int32, (page, page), 0)
                      > lax.broadcasted_iota(jnp.int32, (page, page), 1), 1.0, 0.0).astype(BF16)
    ones_rows = jnp.where(lax.broadcasted_iota(jnp.int32, (BF16_ROWS, page), 0) < BIAS_PIECES, 1.0, 0.0).astype(BF16)
    carry = carry_ref[...]
    lfts = [lf_refs[j][0, 0] for j in range(pp)]
    r3 = _dot(jnp.concatenate([piece for lft in lfts for piece in _split3(lft)], axis=0), later)
    kt_augs = []
    for j in range(pp):
        r = r3[3 * nh * j:3 * nh * (j + 1)]
        suffix = r[0:nh] + r[nh:2 * nh] + r[2 * nh:3 * nh] + carry
        carry = carry + jnp.sum(lfts[j], axis=1, keepdims=True)
        kt_augs.append(jnp.concatenate([k_refs[j][0, 0].astype(BF16), *_split3(suffix), ones_rows], axis=0))
    carry_ref[...] = carry
    half = pp // 2
    halves = [slice(0, half), slice(half, pp)]
    logits = [_dot(qa, jnp.concatenate(kt_augs[h], axis=1)) for h in halves]
    for h, lg in zip(halves, logits):
        update([lg], [v_refs[j][0, 0].astype(BF16) for j in range(pp)[h]])

    @pl.when(i == pl.num_programs(1) - 1)
    def _():
        head_of_row = jnp.right_shift(lax.broadcasted_iota(jnp.int32, (nrow, LANES), 0), int(math.log2(nq)))
        half_of_lane = jnp.right_shift(lax.broadcasted_iota(jnp.int32, (nrow, LANES), 1), int(math.log2(LANES // 2)))
        own_half = jnp.bitwise_and(head_of_row, 1) == half_of_lane
        o_ref[0] = jnp.where(own_half, acc_ref[...] / l_ref[...], 0.0)


def _pieces(x, axis):
    return jnp.stack(_split3(x), axis=axis)


def _decode_attn(q, k_new, v_new, c_new, cache_k, cache_v, cache_logf, page_table, layer, *, pages_per_step):
    db, s, width = q.shape
    _, n_pool, page, nh, head_dim = cache_k.shape
    n_pages = page_table.shape[1]
    nrow = nh * s
    pp = pages_per_step
    assert nrow == LANES and page == LANES and n_pages % pp == 0 and LANES // head_dim == 2
    assert s & (s - 1) == 0 and s <= page and nh == BF16_ROWS
    kt_cache = cache_k.transpose(0, 1, 3, 4, 2).reshape(-1, n_pool, width, page)
    vt_cache = cache_v.transpose(0, 1, 3, 4, 2).reshape(-1, n_pool, width, page)
    lft_cache = cache_logf.transpose(0, 1, 3, 2)
    eye = jnp.eye(nh, dtype=BF16)
    qh = q.reshape(db, s, nh, head_dim).transpose(0, 2, 1, 3)
    qbd = (qh[:, :, :, None, :] * eye[None, :, None, :, None]).reshape(db, nrow, width)
    pick = jnp.broadcast_to(jnp.tile(jnp.repeat(eye, s, axis=0), (1, BIAS_PIECES))[None], (db, nrow, BIAS_PIECES * nh))
    c_rows = c_new.transpose(0, 2, 1).reshape(db, nrow)
    c_cols = jnp.pad(_pieces(c_rows, 2), ((0, 0), (0, 0), (0, BF16_ROWS - BIAS_PIECES)))
    qa = jnp.concatenate([qbd, pick, c_cols], axis=2)
    lane_pad = ((0, 0), (0, 0), (0, page - s))
    neg_c = jnp.pad(-c_new.transpose(0, 2, 1), lane_pad)
    ones = jnp.broadcast_to((jnp.arange(BF16_ROWS) < BIAS_PIECES).astype(BF16)[None, :, None], (db, BF16_ROWS, page))
    knt = jnp.concatenate([jnp.pad(k_new.transpose(0, 2, 1), lane_pad).astype(BF16),
                           _pieces(neg_c, 1).reshape(db, BIAS_PIECES * nh, page), ones], axis=1)
    vnt = jnp.pad(v_new.transpose(0, 2, 1), lane_pad).astype(BF16)
    ka = qa.shape[2]

    def page_map(j):
        return lambda b, i, pt: (layer, pt[b * n_pages + n_pages - 1 - (i * pp + j)], 0, 0)

    per_seq = lambda shape: pl.BlockSpec((1,) + shape, lambda b, i, pt: (b, 0, 0))
    in_specs = ([per_seq((nrow, ka)), per_seq((ka, page)), per_seq((width, page))]
                + [pl.BlockSpec((1, 1, width, page), page_map(j)) for j in range(pp)]
                + [pl.BlockSpec((1, 1, width, page), page_map(j)) for j in range(pp)]
                + [pl.BlockSpec((1, 1, nh, page), page_map(j)) for j in range(pp)])
    out = pl.pallas_call(
        functools.partial(_decode_kernel, pages_per_step=pp),
        grid_spec=pltpu.PrefetchScalarGridSpec(
            num_scalar_prefetch=1,
            grid=(db, n_pages // pp),
            in_specs=in_specs,
            out_specs=pl.BlockSpec((1, nrow, LANES), lambda b, i, pt: (b, 0, 0)),
            scratch_shapes=[pltpu.VMEM((nrow, 1), F32), pltpu.VMEM((nrow, 1), F32),
                            pltpu.VMEM((nrow, LANES), F32), pltpu.VMEM((nh, 1), F32)]),
        out_shape=jax.ShapeDtypeStruct((db, nrow, LANES), F32),
        compiler_params=_params("parallel", "arbitrary"),
        name="decode",
    )(page_table.reshape(-1), qa, knt, vnt, *([kt_cache] * pp), *([vt_cache] * pp), *([lft_cache] * pp))
    o = out.reshape(db, nh // 2, 2, s, 2, head_dim)
    o = jnp.stack([o[:, :, 0, :, 0, :], o[:, :, 1, :, 1, :]], axis=2)
    return o.transpose(0, 3, 1, 2, 4).reshape(db, s, width).astype(BF16)


def _token_tile(n, want):
    tm = min(n, want)
    assert n % tm == 0
    return tm


def kernel(x_prompt, x_sample, cache_k, cache_v, cache_logf, state_conv, page_table, norm_ffn1, w_ffn1_gate, w_ffn1_up, w_ffn1_down, norm_mix, w_in, b_forget, w_dw, b_dw, ln_conv_g, ln_conv_b, w_proj_conv, w_proj_attn, w_out, norm_ffn2, w_ffn2_gate, w_ffn2_up, w_ffn2_down, norm_final):
    depth = norm_ffn1.shape[0]
    nb, t, d = x_prompt.shape
    db, s, _ = x_sample.shape
    nh, head_dim = cache_k.shape[3], cache_k.shape[4]
    aw = nh * head_dim
    cc = w_dw.shape[2]
    taps = w_dw.shape[1]
    n_pool, page = cache_k.shape[1], cache_k.shape[2]
    q_scale = 1.0 / math.sqrt(head_dim)
    assert q_scale == 2.0 ** round(math.log2(q_scale))

    xp, xs = x_prompt, x_sample.reshape(1, db * s, d)
    outs = [[] for _ in range(8)]
    for l in range(depth):
        bf = lambda a: a[l].astype(BF16)
        w1 = (bf(w_ffn1_gate), bf(w_ffn1_up), bf(w_ffn1_down))
        w2 = (bf(w_ffn2_gate), bf(w_ffn2_up), bf(w_ffn2_down))
        wi = w_in[l].astype(BF16)
        edges = [0, cc, 2 * cc, 2 * cc + aw, 2 * cc + 2 * aw, 2 * cc + 3 * aw, 2 * cc + 3 * aw + nh,
                 2 * cc + 3 * aw + nh + d, 2 * cc + 3 * aw + nh + 2 * d]
        w_split = tuple(wi[:, a:b] for a, b in zip(edges[:-1], edges[1:]))
        wpc, wpa, wo = bf(w_proj_conv), bf(w_proj_attn), bf(w_out)
        last = l == depth - 1

        def half_layer(x3, seg, tm_ffn, tm_in, prompt):
            b3, t3, _ = x3.shape
            x1 = _ffn(x3.reshape(b3 * t3, d), norm_ffn1[l], *w1, tm=tm_ffn).reshape(b3, t3, d)
            return x1, _inproj(x1, norm_mix[l], w_split, b_forget[l], tm=tm_in, seg=seg, q_scale=q_scale,
                               prompt=prompt)

        def finish(x1, conv_y, attn, gc, ga, tm):
            n = x1.shape[0] * x1.shape[1]
            merge = (conv_y.reshape(n, d), attn.reshape(n, aw), gc.reshape(n, d), ga.reshape(n, d), wpa, wo)
            return _ffn(x1.reshape(n, d), norm_ffn2[l], *w2, norm_final if last else None, tm=tm, merge=merge)

        tm_p = _token_tile(t, 256)
        x1, (glu, qt, kt, vt32, kb, kx, vt, lf, c, gc, ga) = half_layer(xp, t, _token_tile(nb * t, 512), tm_p, True)
        hist0 = jnp.zeros((nb, CONV_HALO, cc), F32)
        conv_y = _conv(glu, hist0, w_dw[l], b_dw[l], ln_conv_g[l], ln_conv_b[l], wpc, tt=_token_tile(t, 512))
        tq = _token_tile(t, 1024)
        attn = _attn(qt, kb, kx, vt, c, tq=tq, tk=tq // 4, head_dim=head_dim)
        xp = finish(x1, conv_y, attn, gc, ga, _token_tile(nb * t, 256)).reshape(nb, t, d)
        if last:
            y_prompt = xp
        outs[0].append(kt.reshape(nb, nh, head_dim, t).transpose(0, 3, 1, 2))
        outs[1].append(vt32.reshape(nb, nh, head_dim, t).transpose(0, 3, 1, 2))
        outs[2].append(lf)
        outs[3].append(glu[:, t - (taps - 1):])

        n_s = db * s
        x1, (glu, q, k, v, lf, c, gc, ga) = half_layer(xs, s, n_s, n_s, False)
        glu = glu.reshape(db, s, cc)
        state = state_conv[l].astype(F32)
        hist = jnp.pad(state, ((0, 0), (CONV_HALO - (taps - 1), 0), (0, 0)))
        conv_y = _conv(glu, hist, w_dw[l], b_dw[l], ln_conv_g[l], ln_conv_b[l], wpc, tt=s)
        attn = _decode_attn(q.reshape(db, s, aw), k.reshape(db, s, aw), v.reshape(db, s, aw),
                            c.reshape(db, s, nh), cache_k, cache_v, cache_logf, page_table, l,
                            pages_per_step=8)
        xs = finish(x1, conv_y, attn, gc, ga, n_s).reshape(1, n_s, d)
        if last:
            y_sample = xs.reshape(db, s, d)
        outs[4].append(k.reshape(db, s, nh, head_dim))
        outs[5].append(v.reshape(db, s, nh, head_dim))
        outs[6].append(lf.reshape(db, s, nh))
        outs[7].append(jnp.concatenate([state, glu], axis=1)[:, s:])

    stacked = [jnp.stack(o) for o in outs]
    return (y_prompt, y_sample, *stacked)
```

```python
import functools
import math

import jax
import jax.numpy as jnp
from jax import lax
from jax.experimental import pallas as pl
from jax.experimental.pallas import tpu as pltpu

F32 = jnp.float32
BF16 = jnp.bfloat16
EPS = 1e-6
LANES = 128
SUBLANES = 8
BF16_ROWS = 16
VMEM_LIMIT_BYTES = 56 * 2 ** 20
CONV_HALO = 32


def _dot(a, b):
    return jnp.dot(a, b, preferred_element_type=F32)


def _rms(x, g):
    return x * lax.rsqrt(jnp.mean(x * x, axis=-1, keepdims=True) + EPS) * g


def _split3(x):
    hi = x.astype(BF16)
    r = x - hi.astype(F32)
    mid = r.astype(BF16)
    lo = (r - mid.astype(F32)).astype(BF16)
    return hi, mid, lo


def _dot_exact_lhs01(m01, x):
    hi, mid, lo = _split3(x)
    return _dot(m01, hi) + _dot(m01, mid) + _dot(m01, lo)


def _dot_exact_rhs01(x, m01):
    hi, mid, lo = _split3(x)
    return _dot(hi, m01) + _dot(mid, m01) + _dot(lo, m01)


def _const_spec(shape):
    nd = len(shape)
    return pl.BlockSpec(shape, lambda *_: (0,) * nd, pipeline_mode=pl.Buffered(1))


def _params(*semantics):
    return pltpu.CompilerParams(dimension_semantics=semantics, vmem_limit_bytes=VMEM_LIMIT_BYTES)


def _ffn_kernel(*refs, final_norm, merge):
    refs = list(refs)
    x_ref = refs.pop(0)
    if merge:
        conv_ref, attn_ref, gc_ref, ga_ref, wpa_ref, wo_ref = refs[:6]
        del refs[:6]
    if final_norm:
        g_ref, wg_ref, wu_ref, wd_ref, gf_ref, o_ref = refs
    else:
        g_ref, wg_ref, wu_ref, wd_ref, o_ref = refs
    x = x_ref[...]
    if merge:
        mixed = gc_ref[...] * conv_ref[...] + ga_ref[...] * _dot(attn_ref[...], wpa_ref[...])
        x = x + _dot(mixed.astype(BF16), wo_ref[...])
    h = _rms(x, g_ref[...]).astype(BF16)
    gate = _dot(h, wg_ref[...])
    up = _dot(h, wu_ref[...])
    act = (gate * jax.nn.sigmoid(gate) * up).astype(BF16)
    y = x + 0.5 * _dot(act, wd_ref[...])
    if final_norm:
        y = _rms(y, gf_ref[...])
    o_ref[...] = y


def _ffn(x2d, g, wg, wu, wd, gf=None, *, tm, merge=None):
    n, d = x2d.shape
    f = wg.shape[1]
    tile = lambda width: pl.BlockSpec((tm, width), lambda i: (i, 0))
    in_specs, args = [tile(d)], [x2d]
    if merge is not None:
        conv_y, attn, gc, ga, wpa, wo = merge
        in_specs += [tile(d), tile(attn.shape[1]), tile(d), tile(d), _const_spec(wpa.shape), _const_spec(wo.shape)]
        args += [conv_y, attn, gc, ga, wpa, wo]
    in_specs += [_const_spec((1, d)), _const_spec((d, f)), _const_spec((d, f)), _const_spec((f, d))]
    args += [g.reshape(1, d), wg, wu, wd]
    if gf is not None:
        in_specs.append(_const_spec((1, d)))
        args.append(gf.reshape(1, d))
    return pl.pallas_call(
        functools.partial(_ffn_kernel, final_norm=gf is not None, merge=merge is not None),
        grid=(n // tm,),
        in_specs=in_specs,
        out_specs=pl.BlockSpec((tm, d), lambda i: (i, 0)),
        out_shape=jax.ShapeDtypeStruct((n, d), F32),
        compiler_params=_params("parallel"),
        name="ffn",
    )(*args)


LOG2_E = math.log2(math.e)
BIAS_PIECES = 3
KEY_ONES_LANE = 2 * BIAS_PIECES


def _dot_nt(a, b):
    return lax.dot_general(a, b, (((1,), (1,)), ((), ())), preferred_element_type=F32)


def _inproj_kernel(*refs, seg, q_scale, prompt):
    if prompt:
        (x_ref, g_ref, wa_ref, wb_ref, wqt_ref, wkt_ref, wvt_ref, wf_ref, wgc_ref, wga_ref, bf_ref,
         glu_ref, qt_ref, kt_ref, vt_ref, kb_ref, kx_ref, vtb_ref, lf_ref, c_ref, gc_ref, ga_ref, carry_ref) = refs
    else:
        (x_ref, g_ref, wa_ref, wb_ref, wq_ref, wk_ref, wv_ref, wf_ref, wgc_ref, wga_ref, bf_ref,
         glu_ref, q_ref, k_ref, v_ref, lf_ref, c_ref, gc_ref, ga_ref, carry_ref) = refs
    t = pl.program_id(1)
    tm = x_ref.shape[1]
    h = _rms(x_ref[0], g_ref[...]).astype(BF16)
    glu_ref[0] = _dot(h, wa_ref[...]) * jax.nn.sigmoid(_dot(h, wb_ref[...]))
    if prompt:
        qt_ref[0] = (_dot_nt(wqt_ref[...], h) * (q_scale * LOG2_E)).astype(BF16)
        kt = _dot_nt(wkt_ref[...], h)
        kt_ref[0] = kt
        kb_ref[0] = kt.T.astype(BF16)
        vt = _dot_nt(wvt_ref[...], h)
        vt_ref[0] = vt
        vtb_ref[0] = vt.astype(BF16)
    else:
        q_ref[0] = (_dot(h, wq_ref[...]) * q_scale).astype(BF16)
        k_ref[0] = _dot(h, wk_ref[...])
        v_ref[0] = _dot(h, wv_ref[...])
    gc_ref[0] = jax.nn.sigmoid(_dot(h, wgc_ref[...]))
    ga_ref[0] = jax.nn.sigmoid(_dot(h, wga_ref[...]))

    lf = jax.nn.log_sigmoid(_dot(h, wf_ref[...]) + bf_ref[...])
    lf_ref[0] = lf
    row = lax.broadcasted_iota(jnp.int32, (tm, tm), 0)
    col = lax.broadcasted_iota(jnp.int32, (tm, tm), 1)
    keep = col <= row
    if seg < tm:
        shift = int(math.log2(seg))
        keep = keep & (jnp.right_shift(row, shift) == jnp.right_shift(col, shift))
    tri = jnp.where(keep, 1.0, 0.0).astype(BF16)

    @pl.when(t == 0)
    def _():
        carry_ref[...] = jnp.zeros_like(carry_ref)

    c = _dot_exact_lhs01(tri, lf) + carry_ref[...]
    c_ref[0] = c
    carry_ref[...] = c_ref[0, tm - 1:tm, :]

    if prompt:
        nh, width = lf.shape[1], kx_ref.shape[2]
        head = lax.broadcasted_iota(jnp.int32, (nh, width), 0)
        lane = lax.broadcasted_iota(jnp.int32, (nh, width), 1)
        in_pair = jnp.right_shift(lane, int(math.log2(LANES))) == jnp.right_shift(head, 1)
        slot = jnp.bitwise_and(lane, LANES - 1) - jnp.bitwise_and(head, 1) * BIAS_PIECES
        ext = None
        for p, piece in enumerate(_split3(c * LOG2_E)):
            place = jnp.where(in_pair & (slot == p), -1.0, 0.0).astype(BF16)
            term = _dot(piece, place)
            ext = term if ext is None else ext + term
        within = jnp.bitwise_and(lax.broadcasted_iota(jnp.int32, (1, width), 1), LANES - 1)
        ones = jnp.where((within >= KEY_ONES_LANE) & (within < KEY_ONES_LANE + BIAS_PIECES), 1.0, 0.0)
        kx_ref[0] = (ext + ones).astype(BF16)


def _inproj(x3, g, w, b_forget, *, tm, seg, q_scale, prompt):
    nb, t, d = x3.shape
    nt = t // tm
    assert seg >= tm or (nt == 1 and tm % seg == 0 and seg & (seg - 1) == 0)
    wa, wb, wq, wk, wv, wf, wgc, wga = w
    cc, aw, nh = wa.shape[1], wq.shape[1], wf.shape[1]
    tile = lambda width: pl.BlockSpec((1, tm, width), lambda b, i: (b, i, 0))
    tile_t = lambda width: pl.BlockSpec((1, width, tm), lambda b, i: (b, 0, i))
    shape = lambda width, dt: jax.ShapeDtypeStruct((nb, t, width), dt)
    shape_t = lambda width, dt: jax.ShapeDtypeStruct((nb, width, t), dt)
    if prompt:
        assert 2 * (aw // nh) == LANES and KEY_ONES_LANE + BIAS_PIECES <= LANES
        weights = (wa, wb, wq.T, wk.T, wv.T, wf, wgc, wga)
        out_specs = [tile(cc), tile_t(aw), tile_t(aw), tile_t(aw), tile(aw), tile(aw), tile_t(aw),
                     tile(nh), tile(nh), tile(d), tile(d)]
        out_shape = [shape(cc, F32), shape_t(aw, BF16), shape_t(aw, F32), shape_t(aw, F32), shape(aw, BF16),
                     shape(aw, BF16), shape_t(aw, BF16), shape(nh, F32), shape(nh, F32), shape(d, F32), shape(d, F32)]
    else:
        weights = w
        out_specs = [tile(cc), tile(aw), tile(aw), tile(aw), tile(nh), tile(nh), tile(d), tile(d)]
        out_shape = [shape(cc, F32), shape(aw, BF16), shape(aw, F32), shape(aw, F32),
                     shape(nh, F32), shape(nh, F32), shape(d, F32), shape(d, F32)]
    in_specs = [tile(d), _const_spec((1, d))] + [_const_spec(m.shape) for m in weights] + [_const_spec((1, nh))]
    return pl.pallas_call(
        functools.partial(_inproj_kernel, seg=seg, q_scale=q_scale, prompt=prompt),
        grid=(nb, nt),
        in_specs=in_specs,
        out_specs=out_specs,
        out_shape=out_shape,
        scratch_shapes=[pltpu.VMEM((1, nh), F32)],
        compiler_params=_params("parallel", "arbitrary"),
        name="inproj",
    )(x3, g.reshape(1, d), *weights, b_forget.reshape(1, nh))


def _conv_kernel(x_ref, halo_ref, hist_ref, w_ref, b_ref, g_ref, beta_ref, wp_ref, o_ref,
                 buf_ref, sh_ref, y_ref, *, rows):
    i = pl.program_id(1)
    tt, ch = x_ref.shape[1], x_ref.shape[2]
    taps = w_ref.shape[0]
    buf_ref[0:CONV_HALO, :] = jnp.where(i == 0, hist_ref[0], halo_ref[0])
    buf_ref[CONV_HALO:CONV_HALO + tt, :] = x_ref[0]
    first = CONV_HALO - (taps - 1)

    for cb in range(ch // LANES):
        lanes = slice(cb * LANES, (cb + 1) * LANES)
        for r in range(SUBLANES):
            span = CONV_HALO + tt - (SUBLANES if r else 0)
            sh_ref[r, 0:span, :] = buf_ref[r:r + span, lanes]

        def body(rb, carry, lanes=lanes):
            r0 = pl.multiple_of(rb * rows, rows)
            acc = jnp.broadcast_to(b_ref[:, lanes], (rows, LANES))
            for j in range(taps):
                shift, base = (first + j) % SUBLANES, (first + j) // SUBLANES * SUBLANES
                acc = acc + sh_ref[shift, pl.ds(r0 + base, rows), :] * w_ref[j:j + 1, lanes]
            y_ref[pl.ds(r0, rows), lanes] = acc
            return carry

        lax.fori_loop(0, tt // rows, body, 0)

    y = y_ref[...]
    mu = jnp.mean(y, axis=-1, keepdims=True)
    dev = y - mu
    var = jnp.mean(dev * dev, axis=-1, keepdims=True)
    yn = dev * lax.rsqrt(var + EPS) * g_ref[...] + beta_ref[...]
    act = (yn * jax.nn.sigmoid(yn)).astype(BF16)
    o_ref[0] = _dot(act, wp_ref[...])


def _conv(glu, hist, w_dw, b_dw, ln_g, ln_b, wp, *, tt):
    nb, t, ch = glu.shape
    nt = t // tt
    taps = w_dw.shape[0]
    assert taps - 1 <= CONV_HALO and tt % SUBLANES == 0
    if nt > 1:
        assert tt % CONV_HALO == 0
        halo_src = glu
        halo_map = lambda b, i: (b, jnp.maximum(i * (tt // CONV_HALO) - 1, 0), 0)
    else:
        halo_src = hist
        halo_map = lambda b, i: (b, 0, 0)
    rows = min(tt, 64)
    return pl.pallas_call(
        functools.partial(_conv_kernel, rows=rows),
        grid=(nb, nt),
        in_specs=[pl.BlockSpec((1, tt, ch), lambda b, i: (b, i, 0)),
                  pl.BlockSpec((1, CONV_HALO, ch), halo_map),
                  pl.BlockSpec((1, CONV_HALO, ch), lambda b, i: (b, 0, 0)),
                  _const_spec((taps, ch)), _const_spec((1, ch)), _const_spec((1, ch)), _const_spec((1, ch)),
                  _const_spec(wp.shape)],
        out_specs=pl.BlockSpec((1, tt, wp.shape[1]), lambda b, i: (b, i, 0)),
        out_shape=jax.ShapeDtypeStruct((nb, t, wp.shape[1]), F32),
        scratch_shapes=[pltpu.VMEM((CONV_HALO + tt, ch), F32), pltpu.VMEM((SUBLANES, CONV_HALO + tt, LANES), F32),
                        pltpu.VMEM((tt, ch), F32)],
        compiler_params=_params("parallel", "parallel"),
        name="conv",
    )(glu, halo_src, hist, w_dw, b_dw.reshape(1, ch), ln_g.reshape(1, ch), ln_b.reshape(1, ch), wp)


def _attn_kernel(qt_ref, k_ref, kx_ref, vt_ref, cq_ref, o_ref, sa_ref, sb_ref, *, tq, tk, head_dim):
    qi = pl.program_id(2)
    qt2 = qt_ref[0].astype(F32)
    sub = lax.broadcasted_iota(jnp.int32, (LANES, tq), 0)
    qaug = []
    for a in range(2):
        qm = jnp.where((sub >= head_dim) == bool(a), qt2, 0.0)
        ext = jnp.where((sub >= a * BIAS_PIECES) & (sub < (a + 1) * BIAS_PIECES), 1.0, 0.0)
        for p, piece in enumerate(_split3(cq_ref[0, 0, a:a + 1, :] * LOG2_E)):
            ext = jnp.where(sub == KEY_ONES_LANE + p, piece.astype(F32), ext)
        qaug.append(jnp.concatenate([qm, ext], axis=0).astype(BF16))
    n_sub = tq // tk
    ones_rows = jnp.ones((BF16_ROWS, tk), BF16)

    def logits(j, s_ref, lo=0):
        start = pl.multiple_of(j * tk, tk)
        kcat = jnp.concatenate([k_ref[0, pl.ds(start, tk), :], kx_ref[0, pl.ds(start, tk), :]], axis=1)
        for a in range(2):
            s_ref[a, :, lo:] = _dot(kcat, qaug[a][:, lo:])

    def absorb(j, s_ref, state, lo=0, causal=False):
        start = pl.multiple_of(j * tk, tk)
        out = []
        for a in range(2):
            m_all, acc_all = state[a]
            m, acc = m_all[:, lo:], acc_all[:, lo:]
            sa = s_ref[a, :, lo:]
            if causal:
                shape = (tk, tq - lo)
                sa = jnp.where(lax.broadcasted_iota(jnp.int32, shape, 0) <= lax.broadcasted_iota(jnp.int32, shape, 1),
                               sa, -jnp.inf)
            m_new = jnp.maximum(m, jnp.max(sa, axis=0, keepdims=True))
            p = jnp.exp2(sa - m_new).astype(BF16)
            vta = jnp.concatenate([vt_ref[0, a * head_dim:(a + 1) * head_dim, pl.ds(start, tk)], ones_rows], axis=0)
            acc = jnp.exp2(m - m_new) * acc + _dot(vta, p)
            if lo:
                m_new = jnp.concatenate([m_all[:, :lo], m_new], axis=1)
                acc = jnp.concatenate([acc_all[:, :lo], acc], axis=1)
            out.append((m_new, acc))
        return tuple(out)

    bufs = (sa_ref, sb_ref)

    def tile_of_chunks(first, state, diagonal):
        for c in range(n_sub):
            if not diagonal:
                logits(first + c + 1, bufs[(c + 1) % 2])
                state = absorb(first + c, bufs[c % 2], state)
            else:
                if c < n_sub - 1:
                    logits(first + c + 1, bufs[(c + 1) % 2], lo=(c + 1) * tk)
                state = absorb(first + c, bufs[c % 2], state, lo=c * tk, causal=True)
        return state

    init = tuple((jnp.full((1, tq), -jnp.inf, F32), jnp.zeros((head_dim + BF16_ROWS, tq), F32)) for _ in range(2))
    logits(0, sa_ref)
    state = lax.fori_loop(0, qi, lambda jj, st: tile_of_chunks(jj * n_sub, st, False), init)
    (_, acc0), (_, acc1) = tile_of_chunks(qi * n_sub, state, True)
    heads = [acc[:head_dim] / acc[head_dim:head_dim + 1] for acc in (acc0, acc1)]
    o_ref[0] = jnp.concatenate(heads, axis=0).T.astype(o_ref.dtype)


def _attn(qt, k, kx, vt, c, *, tq, tk, head_dim):
    nb, width, t = qt.shape
    nh = width // head_dim
    per = LANES // head_dim
    assert per == 2 and nh % per == 0 and t % tq == 0 and tq % (2 * tk) == 0
    npair = nh // per
    cq = c.reshape(nb, t, npair, per).transpose(0, 2, 3, 1)
    return pl.pallas_call(
        functools.partial(_attn_kernel, tq=tq, tk=tk, head_dim=head_dim),
        grid=(nb, npair, t // tq),
        in_specs=[pl.BlockSpec((1, LANES, tq), lambda b, hp, i: (b, hp, i)),
                  pl.BlockSpec((1, t, LANES), lambda b, hp, i: (b, 0, hp)),
                  pl.BlockSpec((1, t, LANES), lambda b, hp, i: (b, 0, hp)),
                  pl.BlockSpec((1, LANES, t), lambda b, hp, i: (b, hp, 0)),
                  pl.BlockSpec((1, 1, per, tq), lambda b, hp, i: (b, hp, 0, i))],
        out_specs=pl.BlockSpec((1, tq, LANES), lambda b, hp, i: (b, i, hp)),
        out_shape=jax.ShapeDtypeStruct((nb, t, width), BF16),
        scratch_shapes=[pltpu.VMEM((per, tk, tq), F32), pltpu.VMEM((per, tk, tq), F32)],
        compiler_params=_params("parallel", "parallel", "arbitrary"),
        name="attn",
    )(qt, k, kx, vt, cq)


def _decode_kernel(pt_ref, qa_ref, knt_ref, vnt_ref, *rest, pages_per_step):
    pp = pages_per_step
    k_refs, v_refs, lf_refs = rest[:pp], rest[pp:2 * pp], rest[2 * pp:3 * pp]
    o_ref = rest[3 * pp]
    m_ref, l_ref, acc_ref, carry_ref = rest[3 * pp + 1:]
    i = pl.program_id(1)
    page = k_refs[0].shape[3]
    nrow = qa_ref.shape[1]
    nh = lf_refs[0].shape[2]
    nq = nrow // nh
    pair_rows = 2 * nq
    qa = qa_ref[0]

    def update(logits_list, vt_list):
        m_old = m_ref[...]
        m_new = m_old
        for lg in logits_list:
            m_new = jnp.maximum(m_new, jnp.max(lg, axis=1, keepdims=True))
        alpha = jnp.exp(m_old - m_new)
        ps = [jnp.exp(lg - m_new) for lg in logits_list]
        l_new = alpha * l_ref[...]
        for p in ps:
            l_new = l_new + jnp.sum(p, axis=1, keepdims=True)
        l_ref[...] = l_new
        m_ref[...] = m_new
        ps = [p.astype(BF16) for p in ps]
        pcat = ps[0] if len(ps) == 1 else jnp.concatenate(ps, axis=1)
        vcat = vt_list[0] if len(vt_list) == 1 else jnp.concatenate(vt_list, axis=1)
        for j in range(nrow // pair_rows):
            rws = slice(j * pair_rows, (j + 1) * pair_rows)
            pv = _dot_nt(pcat[rws, :], vcat[j * LANES:(j + 1) * LANES, :])
            acc_ref[rws, :] = alpha[rws, :] * acc_ref[rws, :] + pv

    @pl.when(i == 0)
    def _():
        m_ref[...] = jnp.full_like(m_ref, -jnp.inf)
        l_ref[...] = jnp.zeros_like(l_ref)
        acc_ref[...] = jnp.zeros_like(acc_ref)
        carry_ref[...] = jnp.zeros_like(carry_ref)
        kpos = lax.broadcasted_iota(jnp.int32, (nrow, page), 1)
        qpos = jnp.bitwise_and(lax.broadcasted_iota(jnp.int32, (nrow, page), 0), nq - 1)
        update([jnp.where(kpos <= qpos, _dot(qa, knt_ref[0]), -jnp.inf)], [vnt_ref[0]])

    later = jnp.where(lax.broadcasted_iota(jnp.int32, (page, page), 0)
                      > lax.broadcasted_iota(jnp.int32, (page, page), 1), 1.0, 0.0).astype(BF16)
    ones_rows = jnp.where(lax.broadcasted_iota(jnp.int32, (BF16_ROWS, page), 0) < BIAS_PIECES, 1.0, 0.0).astype(BF16)
    carry = carry_ref[...]
    lfts = [lf_refs[j][0, 0] for j in range(pp)]
    r3 = _dot(jnp.concatenate([piece for lft in lfts for piece in _split3(lft)], axis=0), later)
    kt_augs = []
    for j in range(pp):
        r = r3[3 * nh * j:3 * nh * (j + 1)]
        suffix = r[0:nh] + r[nh:2 * nh] + r[2 * nh:3 * nh] + carry
        carry = carry + jnp.sum(lfts[j], axis=1, keepdims=True)
        kt_augs.append(jnp.concatenate([k_refs[j][0, 0].astype(BF16), *_split3(suffix), ones_rows], axis=0))
    carry_ref[...] = carry
    half = pp // 2
    halves = [slice(0, half), slice(half, pp)]
    logits = [_dot(qa, jnp.concatenate(kt_augs[h], axis=1)) for h in halves]
    for h, lg in zip(halves, logits):
        update([lg], [v_refs[j][0, 0].astype(BF16) for j in range(pp)[h]])

    @pl.when(i == pl.num_programs(1) - 1)
    def _():
        head_of_row = jnp.right_shift(lax.broadcasted_iota(jnp.int32, (nrow, LANES), 0), int(math.log2(nq)))
        half_of_lane = jnp.right_shift(lax.broadcasted_iota(jnp.int32, (nrow, LANES), 1), int(math.log2(LANES // 2)))
        own_half = jnp.bitwise_and(head_of_row, 1) == half_of_lane
        o_ref[0] = jnp.where(own_half, acc_ref[...] / l_ref[...], 0.0)


def _pieces(x, axis):
    return jnp.stack(_split3(x), axis=axis)


def _decode_attn(q, k_new, v_new, c_new, cache_k, cache_v, cache_logf, page_table, layer, *, pages_per_step):
    db, s, width = q.shape
    _, n_pool, page, nh, head_dim = cache_k.shape
    n_pages = page_table.shape[1]
    nrow = nh * s
    pp = pages_per_step
    assert nrow == LANES and page == LANES and n_pages % pp == 0 and LANES // head_dim == 2
    assert s & (s - 1) == 0 and s <= page and nh == BF16_ROWS
    kt_cache = cache_k.transpose(0, 1, 3, 4, 2).reshape(-1, n_pool, width, page)
    vt_cache = cache_v.transpose(0, 1, 3, 4, 2).reshape(-1, n_pool, width, page)
    lft_cache = cache_logf.transpose(0, 1, 3, 2)
    eye = jnp.eye(nh, dtype=BF16)
    qh = q.reshape(db, s, nh, head_dim).transpose(0, 2, 1, 3)
    qbd = (qh[:, :, :, None, :] * eye[None, :, None, :, None]).reshape(db, nrow, width)
    pick = jnp.broadcast_to(jnp.tile(jnp.repeat(eye, s, axis=0), (1, BIAS_PIECES))[None], (db, nrow, BIAS_PIECES * nh))
    c_rows = c_new.transpose(0, 2, 1).reshape(db, nrow)
    c_cols = jnp.pad(_pieces(c_rows, 2), ((0, 0), (0, 0), (0, BF16_ROWS - BIAS_PIECES)))
    qa = jnp.concatenate([qbd, pick, c_cols], axis=2)
    lane_pad = ((0, 0), (0, 0), (0, page - s))
    neg_c = jnp.pad(-c_new.transpose(0, 2, 1), lane_pad)
    ones = jnp.broadcast_to((jnp.arange(BF16_ROWS) < BIAS_PIECES).astype(BF16)[None, :, None], (db, BF16_ROWS, page))
    knt = jnp.concatenate([jnp.pad(k_new.transpose(0, 2, 1), lane_pad).astype(BF16),
                           _pieces(neg_c, 1).reshape(db, BIAS_PIECES * nh, page), ones], axis=1)
    vnt = jnp.pad(v_new.transpose(0, 2, 1), lane_pad).astype(BF16)
    ka = qa.shape[2]

    def page_map(j):
        return lambda b, i, pt: (layer, pt[b * n_pages + n_pages - 1 - (i * pp + j)], 0, 0)

    per_seq = lambda shape: pl.BlockSpec((1,) + shape, lambda b, i, pt: (b, 0, 0))
    in_specs = ([per_seq((nrow, ka)), per_seq((ka, page)), per_seq((width, page))]
                + [pl.BlockSpec((1, 1, width, page), page_map(j)) for j in range(pp)]
                + [pl.BlockSpec((1, 1, width, page), page_map(j)) for j in range(pp)]
                + [pl.BlockSpec((1, 1, nh, page), page_map(j)) for j in range(pp)])
    out = pl.pallas_call(
        functools.partial(_decode_kernel, pages_per_step=pp),
        grid_spec=pltpu.PrefetchScalarGridSpec(
            num_scalar_prefetch=1,
            grid=(db, n_pages // pp),
            in_specs=in_specs,
            out_specs=pl.BlockSpec((1, nrow, LANES), lambda b, i, pt: (b, 0, 0)),
            scratch_shapes=[pltpu.VMEM((nrow, 1), F32), pltpu.VMEM((nrow, 1), F32),
                            pltpu.VMEM((nrow, LANES), F32), pltpu.VMEM((nh, 1), F32)]),
        out_shape=jax.ShapeDtypeStruct((db, nrow, LANES), F32),
        compiler_params=_params("parallel", "arbitrary"),
        name="decode",
    )(page_table.reshape(-1), qa, knt, vnt, *([kt_cache] * pp), *([vt_cache] * pp), *([lft_cache] * pp))
    o = out.reshape(db, nh // 2, 2, s, 2, head_dim)
    o = jnp.stack([o[:, :, 0, :, 0, :], o[:, :, 1, :, 1, :]], axis=2)
    return o.transpose(0, 3, 1, 2, 4).reshape(db, s, width).astype(BF16)


def _token_tile(n, want):
    tm = min(n, want)
    assert n % tm == 0
    return tm


def kernel(x_prompt, x_sample, cache_k, cache_v, cache_logf, state_conv, page_table, norm_ffn1, w_ffn1_gate, w_ffn1_up, w_ffn1_down, norm_mix, w_in, b_forget, w_dw, b_dw, ln_conv_g, ln_conv_b, w_proj_conv, w_proj_attn, w_out, norm_ffn2, w_ffn2_gate, w_ffn2_up, w_ffn2_down, norm_final):
    depth = norm_ffn1.shape[0]
    nb, t, d = x_prompt.shape
    db, s, _ = x_sample.shape
    nh, head_dim = cache_k.shape[3], cache_k.shape[4]
    aw = nh * head_dim
    cc = w_dw.shape[2]
    taps = w_dw.shape[1]
    n_pool, page = cache_k.shape[1], cache_k.shape[2]
    q_scale = 1.0 / math.sqrt(head_dim)
    assert q_scale == 2.0 ** round(math.log2(q_scale))

    xp, xs = x_prompt, x_sample.reshape(1, db * s, d)
    outs = [[] for _ in range(8)]
    for l in range(depth):
        bf = lambda a: a[l].astype(BF16)
        w1 = (bf(w_ffn1_gate), bf(w_ffn1_up), bf(w_ffn1_down))
        w2 = (bf(w_ffn2_gate), bf(w_ffn2_up), bf(w_ffn2_down))
        wi = w_in[l].astype(BF16)
        edges = [0, cc, 2 * cc, 2 * cc + aw, 2 * cc + 2 * aw, 2 * cc + 3 * aw, 2 * cc + 3 * aw + nh,
                 2 * cc + 3 * aw + nh + d, 2 * cc + 3 * aw + nh + 2 * d]
        w_split = tuple(wi[:, a:b] for a, b in zip(edges[:-1], edges[1:]))
        wpc, wpa, wo = bf(w_proj_conv), bf(w_proj_attn), bf(w_out)
        last = l == depth - 1

        def half_layer(x3, seg, tm_ffn, tm_in, prompt):
            b3, t3, _ = x3.shape
            x1 = _ffn(x3.reshape(b3 * t3, d), norm_ffn1[l], *w1, tm=tm_ffn).reshape(b3, t3, d)
            return x1, _inproj(x1, norm_mix[l], w_split, b_forget[l], tm=tm_in, seg=seg, q_scale=q_scale,
                               prompt=prompt)

        def finish(x1, conv_y, attn, gc, ga, tm):
            n = x1.shape[0] * x1.shape[1]
            merge = (conv_y.reshape(n, d), attn.reshape(n, aw), gc.reshape(n, d), ga.reshape(n, d), wpa, wo)
            return _ffn(x1.reshape(n, d), norm_ffn2[l], *w2, norm_final if last else None, tm=tm, merge=merge)

        tm_p = _token_tile(t, 256)
        x1, (glu, qt, kt, vt32, kb, kx, vt, lf, c, gc, ga) = half_layer(xp, t, _token_tile(nb * t, 512), tm_p, True)
        hist0 = jnp.zeros((nb, CONV_HALO, cc), F32)
        conv_y = _conv(glu, hist0, w_dw[l], b_dw[l], ln_conv_g[l], ln_conv_b[l], wpc, tt=_token_tile(t, 512))
        tq = _token_tile(t, 1024)
        attn = _attn(qt, kb, kx, vt, c, tq=tq, tk=tq // 4, head_dim=head_dim)
        xp = finish(x1, conv_y, attn, gc, ga, _token_tile(nb * t, 256)).reshape(nb, t, d)
        if last:
            y_prompt = xp
        outs[0].append(kt.reshape(nb, nh, head_dim, t).transpose(0, 3, 1, 2))
        outs[1].append(vt32.reshape(nb, nh, head_dim, t).transpose(0, 3, 1, 2))
        outs[2].append(lf)
        outs[3].append(glu[:, t - (taps - 1):])

        n_s = db * s
        x1, (glu, q, k, v, lf, c, gc, ga) = half_layer(xs, s, n_s, n_s, False)
        glu = glu.reshape(db, s, cc)
        state = state_conv[l].astype(F32)
        hist = jnp.pad(state, ((0, 0), (CONV_HALO - (taps - 1), 0), (0, 0)))
        conv_y = _conv(glu, hist, w_dw[l], b_dw[l], ln_conv_g[l], ln_conv_b[l], wpc, tt=s)
        attn = _decode_attn(q.reshape(db, s, aw), k.reshape(db, s, aw), v.reshape(db, s, aw),
                            c.reshape(db, s, nh), cache_k, cache_v, cache_logf, page_table, l,
                            pages_per_step=8)
        xs = finish(x1, conv_y, attn, gc, ga, n_s).reshape(1, n_s, d)
        if last:
            y_sample = xs.reshape(db, s, d)
        outs[4].append(k.reshape(db, s, nh, head_dim))
        outs[5].append(v.reshape(db, s, nh, head_dim))
        outs[6].append(lf.reshape(db, s, nh))
        outs[7].append(jnp.concatenate([state, glu], axis=1)[:, s:])

    stacked = [jnp.stack(o) for o in outs]
    return (y_prompt, y_sample, *stacked)
```

```python
import functools
import math

import jax
import jax.numpy as jnp
from jax import lax
from jax.experimental import pallas as pl
from jax.experimental.pallas import tpu as pltpu

F32 = jnp.float32
BF16 = jnp.bfloat16
EPS = 1e-6
LANES = 128
SUBLANES = 8
BF16_ROWS = 16
VMEM_LIMIT_BYTES = 56 * 2 ** 20
CONV_HALO = 32


def _dot(a, b):
    return jnp.dot(a, b, preferred_element_type=F32)


def _rms(x, g):
    return x * lax.rsqrt(jnp.mean(x * x, axis=-1, keepdims=True) + EPS) * g


def _split3(x):
    hi = x.astype(BF16)
    r = x - hi.astype(F32)
    mid = r.astype(BF16)
    lo = (r - mid.astype(F32)).astype(BF16)
    return hi, mid, lo


def _dot_exact_lhs01(m01, x):
    hi, mid, lo = _split3(x)
    return _dot(m01, hi) + _dot(m01, mid) + _dot(m01, lo)


def _dot_exact_rhs01(x, m01):
    hi, mid, lo = _split3(x)
    return _dot(hi, m01) + _dot(mid, m01) + _dot(lo, m01)


def _const_spec(shape):
    nd = len(shape)
    return pl.BlockSpec(shape, lambda *_: (0,) * nd, pipeline_mode=pl.Buffered(1))


def _params(*semantics):
    return pltpu.CompilerParams(dimension_semantics=semantics, vmem_limit_bytes=VMEM_LIMIT_BYTES)


def _ffn_kernel(*refs, final_norm, merge):
    refs = list(refs)
    x_ref = refs.pop(0)
    if merge:
        conv_ref, attn_ref, gc_ref, ga_ref, wpa_ref, wo_ref = refs[:6]
        del refs[:6]
    if final_norm:
        g_ref, wg_ref, wu_ref, wd_ref, gf_ref, o_ref = refs
    else:
        g_ref, wg_ref, wu_ref, wd_ref, o_ref = refs
    x = x_ref[...]
    if merge:
        mixed = gc_ref[...] * conv_ref[...] + ga_ref[...] * _dot(attn_ref[...], wpa_ref[...])
        x = x + _dot(mixed.astype(BF16), wo_ref[...])
    h = _rms(x, g_ref[...]).astype(BF16)
    gate = _dot(h, wg_ref[...])
    up = _dot(h, wu_ref[...])
    act = (gate * jax.nn.sigmoid(gate) * up).astype(BF16)
    y = x + 0.5 * _dot(act, wd_ref[...])
    if final_norm:
        y = _rms(y, gf_ref[...])
    o_ref[...] = y


def _ffn(x2d, g, wg, wu, wd, gf=None, *, tm, merge=None):
    n, d = x2d.shape
    f = wg.shape[1]
    tile = lambda width: pl.BlockSpec((tm, width), lambda i: (i, 0))
    in_specs, args = [tile(d)], [x2d]
    if merge is not None:
        conv_y, attn, gc, ga, wpa, wo = merge
        in_specs += [tile(d), tile(attn.shape[1]), tile(d), tile(d), _const_spec(wpa.shape), _const_spec(wo.shape)]
        args += [conv_y, attn, gc, ga, wpa, wo]
    in_specs += [_const_spec((1, d)), _const_spec((d, f)), _const_spec((d, f)), _const_spec((f, d))]
    args += [g.reshape(1, d), wg, wu, wd]
    if gf is not None:
        in_specs.append(_const_spec((1, d)))
        args.append(gf.reshape(1, d))
    return pl.pallas_call(
        functools.partial(_ffn_kernel, final_norm=gf is not None, merge=merge is not None),
        grid=(n // tm,),
        in_specs=in_specs,
        out_specs=pl.BlockSpec((tm, d), lambda i: (i, 0)),
        out_shape=jax.ShapeDtypeStruct((n, d), F32),
        compiler_params=_params("parallel"),
        name="ffn",
    )(*args)


LOG2_E = math.log2(math.e)
BIAS_PIECES = 3
KEY_ONES_LANE = 2 * BIAS_PIECES


def _dot_nt(a, b):
    return lax.dot_general(a, b, (((1,), (1,)), ((), ())), preferred_element_type=F32)


def _inproj_kernel(*refs, seg, q_scale, prompt):
    if prompt:
        (x_ref, g_ref, wa_ref, wb_ref, wqt_ref, wkt_ref, wvt_ref, wf_ref, wgc_ref, wga_ref, bf_ref,
         glu_ref, qt_ref, kt_ref, vt_ref, kb_ref, kx_ref, vtb_ref, lf_ref, c_ref, gc_ref, ga_ref, carry_ref) = refs
    else:
        (x_ref, g_ref, wa_ref, wb_ref, wq_ref, wk_ref, wv_ref, wf_ref, wgc_ref, wga_ref, bf_ref,
         glu_ref, q_ref, k_ref, v_ref, lf_ref, c_ref, gc_ref, ga_ref, carry_ref) = refs
    t = pl.program_id(1)
    tm = x_ref.shape[1]
    h = _rms(x_ref[0], g_ref[...]).astype(BF16)
    glu_ref[0] = _dot(h, wa_ref[...]) * jax.nn.sigmoid(_dot(h, wb_ref[...]))
    if prompt:
        qt_ref[0] = (_dot_nt(wqt_ref[...], h) * (q_scale * LOG2_E)).astype(BF16)
        kt = _dot_nt(wkt_ref[...], h)
        kt_ref[0] = kt
        kb_ref[0] = kt.T.astype(BF16)
        vt = _dot_nt(wvt_ref[...], h)
        vt_ref[0] = vt
        vtb_ref[0] = vt.astype(BF16)
    else:
        q_ref[0] = (_dot(h, wq_ref[...]) * q_scale).astype(BF16)
        k_ref[0] = _dot(h, wk_ref[...])
        v_ref[0] = _dot(h, wv_ref[...])
    gc_ref[0] = jax.nn.sigmoid(_dot(h, wgc_ref[...]))
    ga_ref[0] = jax.nn.sigmoid(_dot(h, wga_ref[...]))

    lf = jax.nn.log_sigmoid(_dot(h, wf_ref[...]) + bf_ref[...])
    lf_ref[0] = lf
    row = lax.broadcasted_iota(jnp.int32, (tm, tm), 0)
    col = lax.broadcasted_iota(jnp.int32, (tm, tm), 1)
    keep = col <= row
    if seg < tm:
        shift = int(math.log2(seg))
        keep = keep & (jnp.right_shift(row, shift) == jnp.right_shift(col, shift))
    tri = jnp.where(keep, 1.0, 0.0).astype(BF16)

    @pl.when(t == 0)
    def _():
        carry_ref[...] = jnp.zeros_like(carry_ref)

    c = _dot_exact_lhs01(tri, lf) + carry_ref[...]
    c_ref[0] = c
    carry_ref[...] = c_ref[0, tm - 1:tm, :]

    if prompt:
        nh, width = lf.shape[1], kx_ref.shape[2]
        row = lax.broadcasted_iota(jnp.int32, (BIAS_PIECES * nh, width), 0)
        lane = lax.broadcasted_iota(jnp.int32, (BIAS_PIECES * nh, width), 1)
        head, piece = jnp.bitwise_and(row, nh - 1), jnp.right_shift(row, int(math.log2(nh)))
        in_pair = jnp.right_shift(lane, int(math.log2(LANES))) == jnp.right_shift(head, 1)
        slot = jnp.bitwise_and(lane, LANES - 1) - jnp.bitwise_and(head, 1) * BIAS_PIECES
        place = jnp.where(in_pair & (slot == piece), -1.0, 0.0).astype(BF16)
        ext = _dot(jnp.concatenate(_split3(c * LOG2_E), axis=1), place)
        within = jnp.bitwise_and(lax.broadcasted_iota(jnp.int32, (1, width), 1), LANES - 1)
        ones = jnp.where((within >= KEY_ONES_LANE) & (within < KEY_ONES_LANE + BIAS_PIECES), 1.0, 0.0)
        kx_ref[0] = (ext + ones).astype(BF16)


def _inproj(x3, g, w, b_forget, *, tm, seg, q_scale, prompt):
    nb, t, d = x3.shape
    nt = t // tm
    assert seg >= tm or (nt == 1 and tm % seg == 0 and seg & (seg - 1) == 0)
    wa, wb, wq, wk, wv, wf, wgc, wga = w
    cc, aw, nh = wa.shape[1], wq.shape[1], wf.shape[1]
    tile = lambda width: pl.BlockSpec((1, tm, width), lambda b, i: (b, i, 0))
    tile_t = lambda width: pl.BlockSpec((1, width, tm), lambda b, i: (b, 0, i))
    shape = lambda width, dt: jax.ShapeDtypeStruct((nb, t, width), dt)
    shape_t = lambda width, dt: jax.ShapeDtypeStruct((nb, width, t), dt)
    if prompt:
        assert 2 * (aw // nh) == LANES and KEY_ONES_LANE + BIAS_PIECES <= LANES
        weights = (wa, wb, wq.T, wk.T, wv.T, wf, wgc, wga)
        out_specs = [tile(cc), tile_t(aw), tile_t(aw), tile_t(aw), tile(aw), tile(aw), tile_t(aw),
                     tile(nh), tile(nh), tile(d), tile(d)]
        out_shape = [shape(cc, F32), shape_t(aw, BF16), shape_t(aw, F32), shape_t(aw, F32), shape(aw, BF16),
                     shape(aw, BF16), shape_t(aw, BF16), shape(nh, F32), shape(nh, F32), shape(d, F32), shape(d, F32)]
    else:
        weights = w
        out_specs = [tile(cc), tile(aw), tile(aw), tile(aw), tile(nh), tile(nh), tile(d), tile(d)]
        out_shape = [shape(cc, F32), shape(aw, BF16), shape(aw, F32), shape(aw, F32),
                     shape(nh, F32), shape(nh, F32), shape(d, F32), shape(d, F32)]
    in_specs = [tile(d), _const_spec((1, d))] + [_const_spec(m.shape) for m in weights] + [_const_spec((1, nh))]
    return pl.pallas_call(
        functools.partial(_inproj_kernel, seg=seg, q_scale=q_scale, prompt=prompt),
        grid=(nb, nt),
        in_specs=in_specs,
        out_specs=out_specs,
        out_shape=out_shape,
        scratch_shapes=[pltpu.VMEM((1, nh), F32)],
        compiler_params=_params("parallel", "arbitrary"),
        name="inproj",
    )(x3, g.reshape(1, d), *weights, b_forget.reshape(1, nh))


def _conv_kernel(x_ref, halo_ref, hist_ref, w_ref, b_ref, g_ref, beta_ref, wp_ref, o_ref,
                 buf_ref, sh_ref, y_ref, *, rows):
    i = pl.program_id(1)
    tt, ch = x_ref.shape[1], x_ref.shape[2]
    taps = w_ref.shape[0]
    buf_ref[0:CONV_HALO, :] = jnp.where(i == 0, hist_ref[0], halo_ref[0])
    buf_ref[CONV_HALO:CONV_HALO + tt, :] = x_ref[0]
    first = CONV_HALO - (taps - 1)

    for cb in range(ch // LANES):
        lanes = slice(cb * LANES, (cb + 1) * LANES)
        for r in range(SUBLANES):
            span = CONV_HALO + tt - (SUBLANES if r else 0)
            sh_ref[r, 0:span, :] = buf_ref[r:r + span, lanes]

        def body(rb, carry, lanes=lanes):
            r0 = pl.multiple_of(rb * rows, rows)
            acc = jnp.broadcast_to(b_ref[:, lanes], (rows, LANES))
            for j in range(taps):
                shift, base = (first + j) % SUBLANES, (first + j) // SUBLANES * SUBLANES
                acc = acc + sh_ref[shift, pl.ds(r0 + base, rows), :] * w_ref[j:j + 1, lanes]
            y_ref[pl.ds(r0, rows), lanes] = acc
            return carry

        lax.fori_loop(0, tt // rows, body, 0)

    y = y_ref[...]
    mu = jnp.mean(y, axis=-1, keepdims=True)
    dev = y - mu
    var = jnp.mean(dev * dev, axis=-1, keepdims=True)
    yn = dev * lax.rsqrt(var + EPS) * g_ref[...] + beta_ref[...]
    act = (yn * jax.nn.sigmoid(yn)).astype(BF16)
    o_ref[0] = _dot(act, wp_ref[...])


def _conv(glu, hist, w_dw, b_dw, ln_g, ln_b, wp, *, tt):
    nb, t, ch = glu.shape
    nt = t // tt
    taps = w_dw.shape[0]
    assert taps - 1 <= CONV_HALO and tt % SUBLANES == 0
    if nt > 1:
        assert tt % CONV_HALO == 0
        halo_src = glu
        halo_map = lambda b, i: (b, jnp.maximum(i * (tt // CONV_HALO) - 1, 0), 0)
    else:
        halo_src = hist
        halo_map = lambda b, i: (b, 0, 0)
    rows = min(tt, 128)
    return pl.pallas_call(
        functools.partial(_conv_kernel, rows=rows),
        grid=(nb, nt),
        in_specs=[pl.BlockSpec((1, tt, ch), lambda b, i: (b, i, 0)),
                  pl.BlockSpec((1, CONV_HALO, ch), halo_map),
                  pl.BlockSpec((1, CONV_HALO, ch), lambda b, i: (b, 0, 0)),
                  _const_spec((taps, ch)), _const_spec((1, ch)), _const_spec((1, ch)), _const_spec((1, ch)),
                  _const_spec(wp.shape)],
        out_specs=pl.BlockSpec((1, tt, wp.shape[1]), lambda b, i: (b, i, 0)),
        out_shape=jax.ShapeDtypeStruct((nb, t, wp.shape[1]), F32),
        scratch_shapes=[pltpu.VMEM((CONV_HALO + tt, ch), F32), pltpu.VMEM((SUBLANES, CONV_HALO + tt, LANES), F32),
                        pltpu.VMEM((tt, ch), F32)],
        compiler_params=_params("parallel", "parallel"),
        name="conv",
    )(glu, halo_src, hist, w_dw, b_dw.reshape(1, ch), ln_g.reshape(1, ch), ln_b.reshape(1, ch), wp)


def _attn_kernel(qt_ref, k_ref, kx_ref, vt_ref, cq_ref, o_ref, sa_ref, sb_ref, *, tq, tk, head_dim):
    qi = pl.program_id(2)
    qt2 = qt_ref[0].astype(F32)
    sub = lax.broadcasted_iota(jnp.int32, (LANES, tq), 0)
    qaug = []
    for a in range(2):
        qm = jnp.where((sub >= head_dim) == bool(a), qt2, 0.0)
        ext = jnp.where((sub >= a * BIAS_PIECES) & (sub < (a + 1) * BIAS_PIECES), 1.0, 0.0)
        for p, piece in enumerate(_split3(cq_ref[0, 0, a:a + 1, :] * LOG2_E)):
            ext = jnp.where(sub == KEY_ONES_LANE + p, piece.astype(F32), ext)
        qaug.append(jnp.concatenate([qm, ext], axis=0).astype(BF16))
    n_sub = tq // tk
    ones_rows = jnp.ones((BF16_ROWS, tk), BF16)

    def logits(j, s_ref, lo=0):
        start = pl.multiple_of(j * tk, tk)
        kcat = jnp.concatenate([k_ref[0, pl.ds(start, tk), :], kx_ref[0, pl.ds(start, tk), :]], axis=1)
        for a in range(2):
            s_ref[a, :, lo:] = _dot(kcat, qaug[a][:, lo:])

    def absorb(j, s_ref, state, lo=0, causal=False):
        start = pl.multiple_of(j * tk, tk)
        out = []
        for a in range(2):
            m_all, acc_all = state[a]
            m, acc = m_all[:, lo:], acc_all[:, lo:]
            sa = s_ref[a, :, lo:]
            if causal:
                shape = (tk, tq - lo)
                sa = jnp.where(lax.broadcasted_iota(jnp.int32, shape, 0) <= lax.broadcasted_iota(jnp.int32, shape, 1),
                               sa, -jnp.inf)
            m_new = jnp.maximum(m, jnp.max(sa, axis=0, keepdims=True))
            p = jnp.exp2(sa - m_new).astype(BF16)
            vta = jnp.concatenate([vt_ref[0, a * head_dim:(a + 1) * head_dim, pl.ds(start, tk)], ones_rows], axis=0)
            acc = jnp.exp2(m - m_new) * acc + _dot(vta, p)
            if lo:
                m_new = jnp.concatenate([m_all[:, :lo], m_new], axis=1)
                acc = jnp.concatenate([acc_all[:, :lo], acc], axis=1)
            out.append((m_new, acc))
        return tuple(out)

    bufs = (sa_ref, sb_ref)

    def tile_of_chunks(first, state, diagonal):
        for c in range(n_sub):
            if not diagonal:
                logits(first + c + 1, bufs[(c + 1) % 2])
                state = absorb(first + c, bufs[c % 2], state)
            else:
                if c < n_sub - 1:
                    logits(first + c + 1, bufs[(c + 1) % 2], lo=(c + 1) * tk)
                state = absorb(first + c, bufs[c % 2], state, lo=c * tk, causal=True)
        return state

    init = tuple((jnp.full((1, tq), -jnp.inf, F32), jnp.zeros((head_dim + BF16_ROWS, tq), F32)) for _ in range(2))
    logits(0, sa_ref)
    state = lax.fori_loop(0, qi, lambda jj, st: tile_of_chunks(jj * n_sub, st, False), init)
    (_, acc0), (_, acc1) = tile_of_chunks(qi * n_sub, state, True)
    heads = [acc[:head_dim] / acc[head_dim:head_dim + 1] for acc in (acc0, acc1)]
    o_ref[0] = jnp.concatenate(heads, axis=0).T.astype(o_ref.dtype)


def _attn(qt, k, kx, vt, c, *, tq, tk, head_dim):
    nb, width, t = qt.shape
    nh = width // head_dim
    per = LANES // head_dim
    assert per == 2 and nh % per == 0 and t % tq == 0 and tq % (2 * tk) == 0
    npair = nh // per
    cq = c.reshape(nb, t, npair, per).transpose(0, 2, 3, 1)
    return pl.pallas_call(
        functools.partial(_attn_kernel, tq=tq, tk=tk, head_dim=head_dim),
        grid=(nb, npair, t // tq),
        in_specs=[pl.BlockSpec((1, LANES, tq), lambda b, hp, i: (b, hp, i)),
                  pl.BlockSpec((1, t, LANES), lambda b, hp, i: (b, 0, hp)),
                  pl.BlockSpec((1, t, LANES), lambda b, hp, i: (b, 0, hp)),
                  pl.BlockSpec((1, LANES, t), lambda b, hp, i: (b, hp, 0)),
                  pl.BlockSpec((1, 1, per, tq), lambda b, hp, i: (b, hp, 0, i))],
        out_specs=pl.BlockSpec((1, tq, LANES), lambda b, hp, i: (b, i, hp)),
        out_shape=jax.ShapeDtypeStruct((nb, t, width), BF16),
        scratch_shapes=[pltpu.VMEM((per, tk, tq), F32), pltpu.VMEM((per, tk, tq), F32)],
        compiler_params=_params("parallel", "parallel", "arbitrary"),
        name="attn",
    )(qt, k, kx, vt, cq)


def _decode_kernel(pt_ref, qa_ref, knt_ref, vnt_ref, *rest, pages_per_step):
    pp = pages_per_step
    k_refs, v_refs, lf_refs = rest[:pp], rest[pp:2 * pp], rest[2 * pp:3 * pp]
    o_ref = rest[3 * pp]
    m_ref, l_ref, acc_ref, carry_ref = rest[3 * pp + 1:]
    i = pl.program_id(1)
    page = k_refs[0].shape[3]
    nrow = qa_ref.shape[1]
    nh = lf_refs[0].shape[2]
    nq = nrow // nh
    pair_rows = 2 * nq
    qa = qa_ref[0]

    def update(logits_list, vt_list):
        m_old = m_ref[...]
        m_new = m_old
        for lg in logits_list:
            m_new = jnp.maximum(m_new, jnp.max(lg, axis=1, keepdims=True))
        alpha = jnp.exp(m_old - m_new)
        ps = [jnp.exp(lg - m_new) for lg in logits_list]
        l_new = alpha * l_ref[...]
        for p in ps:
            l_new = l_new + jnp.sum(p, axis=1, keepdims=True)
        l_ref[...] = l_new
        m_ref[...] = m_new
        ps = [p.astype(BF16) for p in ps]
        pcat = ps[0] if len(ps) == 1 else jnp.concatenate(ps, axis=1)
        vcat = vt_list[0] if len(vt_list) == 1 else jnp.concatenate(vt_list, axis=1)
        for j in range(nrow // pair_rows):
            rws = slice(j * pair_rows, (j + 1) * pair_rows)
            pv = _dot_nt(pcat[rws, :], vcat[j * LANES:(j + 1) * LANES, :])
            acc_ref[rws, :] = alpha[rws, :] * acc_ref[rws, :] + pv

    @pl.when(i == 0)
    def _():
        m_ref[...] = jnp.full_like(m_ref, -jnp.inf)
        l_ref[...] = jnp.zeros_like(l_ref)
        acc_ref[...] = jnp.zeros_like(acc_ref)
        carry_ref[...] = jnp.zeros_like(carry_ref)
        kpos = lax.broadcasted_iota(jnp.int32, (nrow, page), 1)
        qpos = jnp.bitwise_and(lax.broadcasted_iota(jnp.int32, (nrow, page), 0), nq - 1)
        update([jnp.where(kpos <= qpos, _dot(qa, knt_ref[0]), -jnp.inf)], [vnt_ref[0]])

    later = jnp.where(lax.broadcasted_iota(jnp.int32, (page, page), 0)
                      > lax.broadcasted_iota(jnp.int32, (page, page), 1), 1.0, 0.0).astype(BF16)
    ones_rows = jnp.where(lax.broadcasted_iota(jnp.int32, (BF16_ROWS, page), 0) < BIAS_PIECES, 1.0, 0.0).astype(BF16)
    carry = carry_ref[...]
    lfts = [lf_refs[j][0, 0] for j in range(pp)]
    r3 = _dot(jnp.concatenate([piece for lft in lfts for piece in _split3(lft)], axis=0), later)
    kt_augs = []
    for j in range(pp):
        r = r3[3 * nh * j:3 * nh * (j + 1)]
        suffix = r[0:nh] + r[nh:2 * nh] + r[2 * nh:3 * nh] + carry
        carry = carry + jnp.sum(lfts[j], axis=1, keepdims=True)
        kt_augs.append(jnp.concatenate([k_refs[j][0, 0].astype(BF16), *_split3(suffix), ones_rows], axis=0))
    carry_ref[...] = carry
    half = pp // 2
    halves = [slice(0, half), slice(half, pp)]
    logits = [_dot(qa, jnp.concatenate(kt_augs[h], axis=1)) for h in halves]
    for h, lg in zip(halves, logits):
        update([lg], [v_refs[j][0, 0].astype(BF16) for j in range(pp)[h]])

    @pl.when(i == pl.num_programs(1) - 1)
    def _():
        head_of_row = jnp.right_shift(lax.broadcasted_iota(jnp.int32, (nrow, LANES), 0), int(math.log2(nq)))
        half_of_lane = jnp.right_shift(lax.broadcasted_iota(jnp.int32, (nrow, LANES), 1), int(math.log2(LANES // 2)))
        own_half = jnp.bitwise_and(head_of_row, 1) == half_of_lane
        o_ref[0] = jnp.where(own_half, acc_ref[...] / l_ref[...], 0.0)


def _pieces(x, axis):
    return jnp.stack(_split3(x), axis=axis)


def _decode_attn(q, k_new, v_new, c_new, cache_k, cache_v, cache_logf, page_table, layer, *, pages_per_step):
    db, s, width = q.shape
    _, n_pool, page, nh, head_dim = cache_k.shape
    n_pages = page_table.shape[1]
    nrow = nh * s
    pp = pages_per_step
    assert nrow == LANES and page == LANES and n_pages % pp == 0 and LANES // head_dim == 2
    assert s & (s - 1) == 0 and s <= page and nh == BF16_ROWS
    kt_cache = cache_k.transpose(0, 1, 3, 4, 2).reshape(-1, n_pool, width, page)
    vt_cache = cache_v.transpose(0, 1, 3, 4, 2).reshape(-1, n_pool, width, page)
    lft_cache = cache_logf.transpose(0, 1, 3, 2)
    eye = jnp.eye(nh, dtype=BF16)
    qh = q.reshape(db, s, nh, head_dim).transpose(0, 2, 1, 3)
    qbd = (qh[:, :, :, None, :] * eye[None, :, None, :, None]).reshape(db, nrow, width)
    pick = jnp.broadcast_to(jnp.tile(jnp.repeat(eye, s, axis=0), (1, BIAS_PIECES))[None], (db, nrow, BIAS_PIECES * nh))
    c_rows = c_new.transpose(0, 2, 1).reshape(db, nrow)
    c_cols = jnp.pad(_pieces(c_rows, 2), ((0, 0), (0, 0), (0, BF16_ROWS - BIAS_PIECES)))
    qa = jnp.concatenate([qbd, pick, c_cols], axis=2)
    lane_pad = ((0, 0), (0, 0), (0, page - s))
    neg_c = jnp.pad(-c_new.transpose(0, 2, 1), lane_pad)
    ones = jnp.broadcast_to((jnp.arange(BF16_ROWS) < BIAS_PIECES).astype(BF16)[None, :, None], (db, BF16_ROWS, page))
    knt = jnp.concatenate([jnp.pad(k_new.transpose(0, 2, 1), lane_pad).astype(BF16),
                           _pieces(neg_c, 1).reshape(db, BIAS_PIECES * nh, page), ones], axis=1)
    vnt = jnp.pad(v_new.transpose(0, 2, 1), lane_pad).astype(BF16)
    ka = qa.shape[2]

    def page_map(j):
        return lambda b, i, pt: (layer, pt[b * n_pages + n_pages - 1 - (i * pp + j)], 0, 0)

    per_seq = lambda shape: pl.BlockSpec((1,) + shape, lambda b, i, pt: (b, 0, 0))
    in_specs = ([per_seq((nrow, ka)), per_seq((ka, page)), per_seq((width, page))]
                + [pl.BlockSpec((1, 1, width, page), page_map(j)) for j in range(pp)]
                + [pl.BlockSpec((1, 1, width, page), page_map(j)) for j in range(pp)]
                + [pl.BlockSpec((1, 1, nh, page), page_map(j)) for j in range(pp)])
    out = pl.pallas_call(
        functools.partial(_decode_kernel, pages_per_step=pp),
        grid_spec=pltpu.PrefetchScalarGridSpec(
            num_scalar_prefetch=1,
            grid=(db, n_pages // pp),
            in_specs=in_specs,
            out_specs=pl.BlockSpec((1, nrow, LANES), lambda b, i, pt: (b, 0, 0)),
            scratch_shapes=[pltpu.VMEM((nrow, 1), F32), pltpu.VMEM((nrow, 1), F32),
                            pltpu.VMEM((nrow, LANES), F32), pltpu.VMEM((nh, 1), F32)]),
        out_shape=jax.ShapeDtypeStruct((db, nrow, LANES), F32),
        compiler_params=_params("parallel", "arbitrary"),
        name="decode",
    )(page_table.reshape(-1), qa, knt, vnt, *([kt_cache] * pp), *([vt_cache] * pp), *([lft_cache] * pp))
    o = out.reshape(db, nh // 2, 2, s, 2, head_dim)
    o = jnp.stack([o[:, :, 0, :, 0, :], o[:, :, 1, :, 1, :]], axis=2)
    return o.transpose(0, 3, 1, 2, 4).reshape(db, s, width).astype(BF16)


def _token_tile(n, want):
    tm = min(n, want)
    assert n % tm == 0
    return tm


def kernel(x_prompt, x_sample, cache_k, cache_v, cache_logf, state_conv, page_table, norm_ffn1, w_ffn1_gate, w_ffn1_up, w_ffn1_down, norm_mix, w_in, b_forget, w_dw, b_dw, ln_conv_g, ln_conv_b, w_proj_conv, w_proj_attn, w_out, norm_ffn2, w_ffn2_gate, w_ffn2_up, w_ffn2_down, norm_final):
    depth = norm_ffn1.shape[0]
    nb, t, d = x_prompt.shape
    db, s, _ = x_sample.shape
    nh, head_dim = cache_k.shape[3], cache_k.shape[4]
    aw = nh * head_dim
    cc = w_dw.shape[2]
    taps = w_dw.shape[1]
    n_pool, page = cache_k.shape[1], cache_k.shape[2]
    q_scale = 1.0 / math.sqrt(head_dim)
    assert q_scale == 2.0 ** round(math.log2(q_scale))

    xp, xs = x_prompt, x_sample.reshape(1, db * s, d)
    outs = [[] for _ in range(8)]
    for l in range(depth):
        bf = lambda a: a[l].astype(BF16)
        w1 = (bf(w_ffn1_gate), bf(w_ffn1_up), bf(w_ffn1_down))
        w2 = (bf(w_ffn2_gate), bf(w_ffn2_up), bf(w_ffn2_down))
        wi = w_in[l].astype(BF16)
        edges = [0, cc, 2 * cc, 2 * cc + aw, 2 * cc + 2 * aw, 2 * cc + 3 * aw, 2 * cc + 3 * aw + nh,
                 2 * cc + 3 * aw + nh + d, 2 * cc + 3 * aw + nh + 2 * d]
        w_split = tuple(wi[:, a:b] for a, b in zip(edges[:-1], edges[1:]))
        wpc, wpa, wo = bf(w_proj_conv), bf(w_proj_attn), bf(w_out)
        last = l == depth - 1

        def half_layer(x3, seg, tm_ffn, tm_in, prompt):
            b3, t3, _ = x3.shape
            x1 = _ffn(x3.reshape(b3 * t3, d), norm_ffn1[l], *w1, tm=tm_ffn).reshape(b3, t3, d)
            return x1, _inproj(x1, norm_mix[l], w_split, b_forget[l], tm=tm_in, seg=seg, q_scale=q_scale,
                               prompt=prompt)

        def finish(x1, conv_y, attn, gc, ga, tm):
            n = x1.shape[0] * x1.shape[1]
            merge = (conv_y.reshape(n, d), attn.reshape(n, aw), gc.reshape(n, d), ga.reshape(n, d), wpa, wo)
            return _ffn(x1.reshape(n, d), norm_ffn2[l], *w2, norm_final if last else None, tm=tm, merge=merge)

        tm_p = _token_tile(t, 256)
        x1, (glu, qt, kt, vt32, kb, kx, vt, lf, c, gc, ga) = half_layer(xp, t, _token_tile(nb * t, 512), tm_p, True)
        hist0 = jnp.zeros((nb, CONV_HALO, cc), F32)
        conv_y = _conv(glu, hist0, w_dw[l], b_dw[l], ln_conv_g[l], ln_conv_b[l], wpc, tt=_token_tile(t, 512))
        tq = _token_tile(t, 1024)
        attn = _attn(qt, kb, kx, vt, c, tq=tq, tk=tq // 4, head_dim=head_dim)
        xp = finish(x1, conv_y, attn, gc, ga, _token_tile(nb * t, 256)).reshape(nb, t, d)
        if last:
            y_prompt = xp
        outs[0].append(kt.reshape(nb, nh, head_dim, t).transpose(0, 3, 1, 2))
        outs[1].append(vt32.reshape(nb, nh, head_dim, t).transpose(0, 3, 1, 2))
        outs[2].append(lf)
        outs[3].append(glu[:, t - (taps - 1):])

        n_s = db * s
        x1, (glu, q, k, v, lf, c, gc, ga) = half_layer(xs, s, n_s, n_s, False)
        glu = glu.reshape(db, s, cc)
        state = state_conv[l].astype(F32)
        hist = jnp.pad(state, ((0, 0), (CONV_HALO - (taps - 1), 0), (0, 0)))
        conv_y = _conv(glu, hist, w_dw[l], b_dw[l], ln_conv_g[l], ln_conv_b[l], wpc, tt=s)
        attn = _decode_attn(q.reshape(db, s, aw), k.reshape(db, s, aw), v.reshape(db, s, aw),
                            c.reshape(db, s, nh), cache_k, cache_v, cache_logf, page_table, l,
                            pages_per_step=16)
        xs = finish(x1, conv_y, attn, gc, ga, n_s).reshape(1, n_s, d)
        if last:
            y_sample = xs.reshape(db, s, d)
        outs[4].append(k.reshape(db, s, nh, head_dim))
        outs[5].append(v.reshape(db, s, nh, head_dim))
        outs[6].append(lf.reshape(db, s, nh))
        outs[7].append(jnp.concatenate([state, glu], axis=1)[:, s:])

    stacked = [jnp.stack(o) for o in outs]
    return (y_prompt, y_sample, *stacked)
```
